```python
import math
import jax
import jax.numpy as jnp
from jax import lax
import numpy as np

D_MODEL = 1024
BATCH = 4
SEQ = 4096
DEPTH = 2
DEC_BATCH = 128
DEC_SEQ = 4
PAST_LEN = 16384
PAGE_SIZE = 128

MLA_HEADS = 4
MLA_NOPE_DIM = 128
MLA_ROPE_DIM = 64
MLA_V_DIM = 128
Q_LORA_RANK = 256
KV_LORA_RANK = 128
ROPE_BASE = 10000.0
MLA_WIDTH = MLA_HEADS * MLA_V_DIM

SB_HEADS = 4
SB_HEAD_DIM = 64
SB_WIDTH = SB_HEADS * SB_HEAD_DIM

SSM_WIDTH = 256
SSM_GROUP_CH = 16
SSM_GROUPS = SSM_WIDTH // SSM_GROUP_CH
SSM_STATE = 64
DT_MIN = 0.001
DT_MAX = 0.1

MIX_WIDTH = MLA_WIDTH + SB_WIDTH + SSM_WIDTH
D_FF = 2816
Q_BLOCK = 128
N_MOD = 9
MACARON_WEIGHT = 0.5
DEEPNORM_ALPHA = (2 * DEPTH) ** 0.25
DEEPNORM_BETA = (8 * DEPTH) ** -0.25
LN_EPS = 1e-5
RMS_EPS = 1e-6

_IN_SIZES = (Q_LORA_RANK, KV_LORA_RANK, MLA_ROPE_DIM, SB_WIDTH, SB_WIDTH, SB_WIDTH, SSM_WIDTH)
IN_SPLITS = tuple(int(v) for v in np.cumsum(_IN_SIZES)[:-1])
N_IN = int(sum(_IN_SIZES))

kernel_name = 'hybrid_mla_stickbreak_s5_step'


def rms_norm(x, g):
    xf = x.astype(jnp.float32)
    y = xf * lax.rsqrt(jnp.mean(xf * xf, axis=-1, keepdims=True) + RMS_EPS)
    return (y * g.astype(jnp.float32)).astype(x.dtype)


def layer_norm(x, g, b):
    xf = x.astype(jnp.float32)
    mu = jnp.mean(xf, axis=-1, keepdims=True)
    xc = xf - mu
    var = jnp.mean(xc * xc, axis=-1, keepdims=True)
    y = xc * lax.rsqrt(var + LN_EPS) * g.astype(jnp.float32) + b.astype(jnp.float32)
    return y.astype(x.dtype)


def rope(x, pos):
    half = x.shape[-1] // 2
    freqs = ROPE_BASE ** (-jnp.arange(half, dtype=jnp.float32) / half)
    ang = pos.astype(jnp.float32)[:, None] * freqs[None, :]
    ang = ang.reshape((ang.shape[0],) + (1,) * (x.ndim - 3) + (half,))
    cos, sin = jnp.cos(ang), jnp.sin(ang)
    xf = x.astype(jnp.float32)
    x1, x2 = xf[..., :half], xf[..., half:]
    return jnp.concatenate([x1 * cos - x2 * sin, x1 * sin + x2 * cos], axis=-1).astype(x.dtype)


def swiglu(h, w_gate, w_up, w_down):
    return (jax.nn.silu(h @ w_gate) * (h @ w_up)) @ w_down


def query_blocks(n):
    return [(q0, min(q0 + Q_BLOCK, n)) for q0 in range(0, n, Q_BLOCK)]


def weighted_sum(spec, w, values):
    out, start = None, 0
    for val in values:
        n = val.shape[1]
        term = jnp.einsum(spec, w[..., start:start + n].astype(val.dtype), val)
        out = term if out is None else out + term
        start += n
    return out


def mla_attend(q_lat, q_rope, ckv, krope, pos0, past):
    scale = (MLA_NOPE_DIM + MLA_ROPE_DIM) ** -0.5
    outs = []
    for q0, q1 in query_blocks(q_lat.shape[1]):
        qpos = pos0 + jnp.arange(q0, q1)
        segs = list(past) + [(ckv[:, :q1], krope[:, :q1], pos0 + jnp.arange(q1))]
        ql, qr = q_lat[:, q0:q1], q_rope[:, q0:q1]
        scores = []
        for kc, kr, kpos in segs:
            s = (jnp.einsum('bthc,bsc->bhts', ql, kc) + jnp.einsum('bthr,bsr->bhts', qr, kr)).astype(jnp.float32) * scale
            scores.append(jnp.where(kpos[None, :] <= qpos[:, None], s, -jnp.inf))
        prob = jax.nn.softmax(jnp.concatenate(scores, axis=-1), axis=-1)
        outs.append(weighted_sum('bhts,bsc->bthc', prob, [sg[0] for sg in segs]))
    return jnp.concatenate(outs, axis=1)


def stick_breaking_weights(z, allowed):
    log_beta = jax.nn.log_sigmoid(z)
    log_keep = jnp.where(allowed, jax.nn.log_sigmoid(-z), 0.0)
    later = lax.cumsum(log_keep, axis=z.ndim - 1, reverse=True) - log_keep
    return jnp.where(allowed, jnp.exp(log_beta + later), 0.0)


def sb_attend(q, k, v, pos0, past):
    scale = SB_HEAD_DIM ** -0.5
    outs = []
    for q0, q1 in query_blocks(q.shape[1]):
        qpos = pos0 + jnp.arange(q0, q1)
        segs = list(past) + [(k[:, :q1], v[:, :q1], pos0 + jnp.arange(q1))]
        qb = q[:, q0:q1]
        z = jnp.concatenate([jnp.einsum('bthd,bshd->bhts', qb, sk).astype(jnp.float32) for sk, _, _ in segs], axis=-1) * scale
        allowed = jnp.concatenate([kpos[None, :] < qpos[:, None] for _, _, kpos in segs], axis=-1)
        w = stick_breaking_weights(z, allowed)
        outs.append(weighted_sum('bhts,bshd->bthd', w, [sg[1] for sg in segs]))
    return jnp.concatenate(outs, axis=1)


def complex_affine_combine(e1, e2):
    a1r, a1i, b1r, b1i = e1
    a2r, a2i, b2r, b2i = e2
    return (a2r * a1r - a2i * a1i, a2r * a1i + a2i * a1r,
            a2r * b1r - a2i * b1i + b2r, a2r * b1i + a2i * b1r + b2i)


def s5_ssm(u, lam_re, lam_im, log_dt, b_re, b_im, c_re, c_im, d, h0):
    f32 = jnp.float32
    uf = u.astype(f32)
    lr, li = lam_re.astype(f32), lam_im.astype(f32)
    dt = jnp.exp(log_dt.astype(f32))[:, None]
    mag = jnp.exp(lr * dt)
    a_re, a_im = mag * jnp.cos(li * dt), mag * jnp.sin(li * dt)
    den = lr * lr + li * li
    z_re = ((a_re - 1.0) * lr + a_im * li) / den
    z_im = (a_im * lr - (a_re - 1.0) * li) / den
    br, bi = b_re.astype(f32), b_im.astype(f32)
    bb_re = z_re[..., None] * br - z_im[..., None] * bi
    bb_im = z_re[..., None] * bi + z_im[..., None] * br
    x_re = jnp.einsum('gph,btgh->btgp', bb_re, uf)
    x_im = jnp.einsum('gph,btgh->btgp', bb_im, uf)
    if h0 is not None:
        h0_re, h0_im = h0[0].astype(f32), h0[1].astype(f32)
        x_re = x_re.at[:, 0].add(a_re * h0_re - a_im * h0_im)
        x_im = x_im.at[:, 0].add(a_re * h0_im + a_im * h0_re)
    elems = (jnp.broadcast_to(a_re, x_re.shape), jnp.broadcast_to(a_im, x_im.shape), x_re, x_im)
    _, _, h_re, h_im = lax.associative_scan(complex_affine_combine, elems, axis=1)
    y = (jnp.einsum('ghp,btgp->btgh', c_re.astype(f32), h_re)
         - jnp.einsum('ghp,btgp->btgh', c_im.astype(f32), h_im)
         + d.astype(f32) * uf)
    return y.astype(u.dtype), h_re[:, -1], h_im[:, -1]


def mixer(h, p, pos0, past):
    b, t, _ = h.shape
    pos = pos0 + jnp.arange(t)
    cq, ckv_raw, kr_raw, sq, sk, sv, u = jnp.split(h @ p['w_in'], IN_SPLITS, axis=-1)
    cq = rms_norm(cq, p['mla_q_norm'])
    q = jnp.einsum('btc,chd->bthd', cq, p['mla_w_uq'])
    q_nope = q[..., :MLA_NOPE_DIM]
    q_rope = rope(q[..., MLA_NOPE_DIM:], pos)
    ckv = rms_norm(ckv_raw, p['mla_kv_norm'])
    krope = rope(kr_raw, pos)
    q_lat = jnp.einsum('bthd,chd->bthc', q_nope, p['mla_w_uk'])
    o_lat = mla_attend(q_lat, q_rope, ckv, krope, pos0, past['mla'])
    o_mla = jnp.einsum('bthc,chd->bthd', o_lat, p['mla_w_uv']).reshape(b, t, MLA_WIDTH)
    sq = sq.reshape(b, t, SB_HEADS, SB_HEAD_DIM)
    sk = sk.reshape(b, t, SB_HEADS, SB_HEAD_DIM)
    sv = sv.reshape(b, t, SB_HEADS, SB_HEAD_DIM)
    o_sb = sb_attend(sq, sk, sv, pos0, past['sb']).reshape(b, t, SB_WIDTH)
    y_ssm, h_re, h_im = s5_ssm(u.reshape(b, t, SSM_GROUPS, SSM_GROUP_CH), p['ssm_lam_re'], p['ssm_lam_im'],
                               p['ssm_log_dt'], p['ssm_b_re'], p['ssm_b_im'], p['ssm_c_re'], p['ssm_c_im'],
                               p['ssm_d'], past['h0'])
    yg = jax.nn.gelu(y_ssm.reshape(b, t, SSM_WIDTH))
    o_ssm = yg * jax.nn.sigmoid(yg @ p['ssm_w_glu'] + p['ssm_b_glu'])
    merged = jnp.concatenate([rms_norm(o_mla, p['out_norm_mla']), rms_norm(o_sb, p['out_norm_sb']),
                              rms_norm(o_ssm, p['out_norm_ssm'])], axis=-1)
    return merged @ p['w_out'], (ckv, krope, sk, sv, h_re, h_im)


def trunk_layer(x, c, p, pos0, past):
    mod = (jax.nn.silu(c) @ p['ada_w'] + p['ada_b'])[:, None, :]
    sh1, sc1, g1, sh2, sc2, g2, sh3, sc3, g3 = jnp.split(mod, N_MOD, axis=-1)
    f1 = swiglu(x * (1 + sc1) + sh1, p['ffn1_w_gate'], p['ffn1_w_up'], p['ffn1_w_down'])
    x = layer_norm(DEEPNORM_ALPHA * x + MACARON_WEIGHT * (1 + g1) * f1, p['ln_g'][0], p['ln_b'][0])
    m, state = mixer(x * (1 + sc2) + sh2, p, pos0, past)
    x = layer_norm(DEEPNORM_ALPHA * x + (1 + g2) * m, p['ln_g'][1], p['ln_b'][1])
    f2 = swiglu(x * (1 + sc3) + sh3, p['ffn2_w_gate'], p['ffn2_w_up'], p['ffn2_w_down'])
    x = layer_norm(DEEPNORM_ALPHA * x + MACARON_WEIGHT * (1 + g3) * f2, p['ln_g'][2], p['ln_b'][2])
    return x, state


def gather_pages(cache, page_table, layer):
    rows = cache[page_table, layer]
    return rows.reshape((rows.shape[0], rows.shape[1] * rows.shape[2]) + rows.shape[3:])


def stack_states(states, i):
    return jnp.stack([s[i] for s in states], axis=1)


def setup_inputs(seed: int = 0) -> dict:
    key = jax.random.key(seed)
    keys = jax.random.split(key, 64)
    counter = [0]
    f32 = jnp.float32

    def nk():
        counter[0] += 1
        return keys[counter[0] - 1]

    def nrm(shape, scale=1.0):
        return jax.random.normal(nk(), shape, f32) * scale

    def gain(shape):
        return 1.0 + nrm(shape, 0.01)

    n_pages = PAST_LEN // PAGE_SIZE
    n_used = DEC_BATCH * n_pages
    n_pool = n_used + max(1, n_used // 4)
    x_prompt = nrm((BATCH, SEQ, D_MODEL))
    x_sample = nrm((DEC_BATCH, DEC_SEQ, D_MODEL))
    c_prompt = nrm((BATCH, D_MODEL))
    c_sample = nrm((DEC_BATCH, D_MODEL))
    page_table = jax.random.permutation(nk(), n_pool)[:n_used].reshape(DEC_BATCH, n_pages).astype(jnp.int32)
    cache_ckv = nrm((n_pool, DEPTH, PAGE_SIZE, KV_LORA_RANK))
    cache_krope = nrm((n_pool, DEPTH, PAGE_SIZE, MLA_ROPE_DIM))
    cache_sb_k = nrm((n_pool, DEPTH, PAGE_SIZE, SB_HEADS, SB_HEAD_DIM))
    cache_sb_v = nrm((n_pool, DEPTH, PAGE_SIZE, SB_HEADS, SB_HEAD_DIM))
    state_ssm_re = nrm((DEC_BATCH, DEPTH, SSM_GROUPS, SSM_STATE), 0.3)
    state_ssm_im = nrm((DEC_BATCH, DEPTH, SSM_GROUPS, SSM_STATE), 0.3)
    ada_w = nrm((DEPTH, D_MODEL, N_MOD * D_MODEL), 0.2 * D_MODEL ** -0.5)
    ada_b = nrm((DEPTH, N_MOD * D_MODEL), 0.01)
    ffn1_w_gate = nrm((DEPTH, D_MODEL, D_FF), D_MODEL ** -0.5)
    ffn1_w_up = nrm((DEPTH, D_MODEL, D_FF), D_MODEL ** -0.5)
    ffn1_w_down = nrm((DEPTH, D_FF, D_MODEL), DEEPNORM_BETA * D_FF ** -0.5)
    ffn2_w_gate = nrm((DEPTH, D_MODEL, D_FF), D_MODEL ** -0.5)
    ffn2_w_up = nrm((DEPTH, D_MODEL, D_FF), D_MODEL ** -0.5)
    ffn2_w_down = nrm((DEPTH, D_FF, D_MODEL), DEEPNORM_BETA * D_FF ** -0.5)
    ln_g = gain((DEPTH, 3, D_MODEL))
    ln_b = nrm((DEPTH, 3, D_MODEL), 0.01)
    w_in = nrm((DEPTH, D_MODEL, N_IN), D_MODEL ** -0.5)
    mla_q_norm = gain((DEPTH, Q_LORA_RANK))
    mla_kv_norm = gain((DEPTH, KV_LORA_RANK))
    mla_w_uq = nrm((DEPTH, Q_LORA_RANK, MLA_HEADS, MLA_NOPE_DIM + MLA_ROPE_DIM), Q_LORA_RANK ** -0.5)
    mla_w_uk = nrm((DEPTH, KV_LORA_RANK, MLA_HEADS, MLA_NOPE_DIM), KV_LORA_RANK ** -0.5)
    mla_w_uv = nrm((DEPTH, KV_LORA_RANK, MLA_HEADS, MLA_V_DIM), KV_LORA_RANK ** -0.5)
    ssm_lam_re = -0.5 + nrm((DEPTH, SSM_GROUPS, SSM_STATE), 0.01)
    ssm_lam_im = math.pi * jnp.arange(SSM_STATE, dtype=f32) + nrm((DEPTH, SSM_GROUPS, SSM_STATE), 0.01)
    ssm_log_dt = jax.random.uniform(nk(), (DEPTH, SSM_GROUPS), f32, math.log(DT_MIN), math.log(DT_MAX))
    ssm_b_re = nrm((DEPTH, SSM_GROUPS, SSM_STATE, SSM_GROUP_CH), (2 * SSM_GROUP_CH) ** -0.5)
    ssm_b_im = nrm((DEPTH, SSM_GROUPS, SSM_STATE, SSM_GROUP_CH), (2 * SSM_GROUP_CH) ** -0.5)
    ssm_c_re = nrm((DEPTH, SSM_GROUPS, SSM_GROUP_CH, SSM_STATE), (2 * SSM_STATE) ** -0.5)
    ssm_c_im = nrm((DEPTH, SSM_GROUPS, SSM_GROUP_CH, SSM_STATE), (2 * SSM_STATE) ** -0.5)
    ssm_d = nrm((DEPTH, SSM_GROUPS, SSM_GROUP_CH))
    ssm_w_glu = nrm((DEPTH, SSM_WIDTH, SSM_WIDTH), SSM_WIDTH ** -0.5)
    ssm_b_glu = nrm((DEPTH, SSM_WIDTH), 0.01)
    out_norm_mla = gain((DEPTH, MLA_WIDTH))
    out_norm_sb = gain((DEPTH, SB_WIDTH))
    out_norm_ssm = gain((DEPTH, SSM_WIDTH))
    w_out = nrm((DEPTH, MIX_WIDTH, D_MODEL), DEEPNORM_BETA * MIX_WIDTH ** -0.5)
    return {
        'x_prompt': x_prompt, 'x_sample': x_sample, 'c_prompt': c_prompt, 'c_sample': c_sample,
        'page_table': page_table, 'cache_ckv': cache_ckv, 'cache_krope': cache_krope,
        'cache_sb_k': cache_sb_k, 'cache_sb_v': cache_sb_v,
        'state_ssm_re': state_ssm_re, 'state_ssm_im': state_ssm_im,
        'ada_w': ada_w, 'ada_b': ada_b,
        'ffn1_w_gate': ffn1_w_gate, 'ffn1_w_up': ffn1_w_up, 'ffn1_w_down': ffn1_w_down,
        'ffn2_w_gate': ffn2_w_gate, 'ffn2_w_up': ffn2_w_up, 'ffn2_w_down': ffn2_w_down,
        'ln_g': ln_g, 'ln_b': ln_b, 'w_in': w_in,
        'mla_q_norm': mla_q_norm, 'mla_kv_norm': mla_kv_norm,
        'mla_w_uq': mla_w_uq, 'mla_w_uk': mla_w_uk, 'mla_w_uv': mla_w_uv,
        'ssm_lam_re': ssm_lam_re, 'ssm_lam_im': ssm_lam_im, 'ssm_log_dt': ssm_log_dt,
        'ssm_b_re': ssm_b_re, 'ssm_b_im': ssm_b_im, 'ssm_c_re': ssm_c_re, 'ssm_c_im': ssm_c_im,
        'ssm_d': ssm_d, 'ssm_w_glu': ssm_w_glu, 'ssm_b_glu': ssm_b_glu,
        'out_norm_mla': out_norm_mla, 'out_norm_sb': out_norm_sb, 'out_norm_ssm': out_norm_ssm,
        'w_out': w_out,
    }


def reference(x_prompt, x_sample, c_prompt, c_sample, page_table, cache_ckv, cache_krope, cache_sb_k,
              cache_sb_v, state_ssm_re, state_ssm_im, ada_w, ada_b, ffn1_w_gate, ffn1_w_up, ffn1_w_down,
              ffn2_w_gate, ffn2_w_up, ffn2_w_down, ln_g, ln_b, w_in, mla_q_norm, mla_kv_norm, mla_w_uq,
              mla_w_uk, mla_w_uv, ssm_lam_re, ssm_lam_im, ssm_log_dt, ssm_b_re, ssm_b_im, ssm_c_re, ssm_c_im,
              ssm_d, ssm_w_glu, ssm_b_glu, out_norm_mla, out_norm_sb, out_norm_ssm, w_out):
    past_len = page_table.shape[1] * PAGE_SIZE
    past_pos = jnp.arange(past_len)
    prompt_past = {'mla': [], 'sb': [], 'h0': None}
    yp, ys = x_prompt, x_sample
    new_p, new_s = [], []
    for l in range(DEPTH):
        p = {
            'ada_w': ada_w[l], 'ada_b': ada_b[l],
            'ffn1_w_gate': ffn1_w_gate[l], 'ffn1_w_up': ffn1_w_up[l], 'ffn1_w_down': ffn1_w_down[l],
            'ffn2_w_gate': ffn2_w_gate[l], 'ffn2_w_up': ffn2_w_up[l], 'ffn2_w_down': ffn2_w_down[l],
            'ln_g': ln_g[l], 'ln_b': ln_b[l], 'w_in': w_in[l],
            'mla_q_norm': mla_q_norm[l], 'mla_kv_norm': mla_kv_norm[l],
            'mla_w_uq': mla_w_uq[l], 'mla_w_uk': mla_w_uk[l], 'mla_w_uv': mla_w_uv[l],
            'ssm_lam_re': ssm_lam_re[l], 'ssm_lam_im': ssm_lam_im[l], 'ssm_log_dt': ssm_log_dt[l],
            'ssm_b_re': ssm_b_re[l], 'ssm_b_im': ssm_b_im[l], 'ssm_c_re': ssm_c_re[l], 'ssm_c_im': ssm_c_im[l],
            'ssm_d': ssm_d[l], 'ssm_w_glu': ssm_w_glu[l], 'ssm_b_glu': ssm_b_glu[l],
            'out_norm_mla': out_norm_mla[l], 'out_norm_sb': out_norm_sb[l], 'out_norm_ssm': out_norm_ssm[l],
            'w_out': w_out[l],
        }
        sample_past = {
            'mla': [(gather_pages(cache_ckv, page_table, l), gather_pages(cache_krope, page_table, l), past_pos)],
            'sb': [(gather_pages(cache_sb_k, page_table, l), gather_pages(cache_sb_v, page_table, l), past_pos)],
            'h0': (state_ssm_re[:, l], state_ssm_im[:, l]),
        }
        yp, st_p = trunk_layer(yp, c_prompt, p, 0, prompt_past)
        ys, st_s = trunk_layer(ys, c_sample, p, past_len, sample_past)
        new_p.append(st_p)
        new_s.append(st_s)
    return (yp, ys,
            stack_states(new_p, 0), stack_states(new_p, 1), stack_states(new_p, 2),
            stack_states(new_p, 3), stack_states(new_p, 4), stack_states(new_p, 5),
            stack_states(new_s, 0), stack_states(new_s, 1), stack_states(new_s, 2),
            stack_states(new_s, 3), stack_states(new_s, 4), stack_states(new_s, 5))
```

```python
import functools
import math

import jax
import jax.numpy as jnp
import numpy as np
from jax import lax
from jax.experimental import pallas as pl
from jax.experimental.pallas import tpu as pltpu

F32 = jnp.float32
BF16 = jnp.bfloat16

D_MODEL = 1024
DEPTH = 2
PAGE_SIZE = 128
MLA_HEADS = 4
MLA_NOPE_DIM = 128
MLA_ROPE_DIM = 64
MLA_V_DIM = 128
Q_LORA_RANK = 256
KV_LORA_RANK = 128
ROPE_BASE = 10000.0
SB_HEADS = 4
SB_HEAD_DIM = 64
SB_WIDTH = SB_HEADS * SB_HEAD_DIM
SSM_WIDTH = 256
SSM_GROUP_CH = 16
SSM_GROUPS = SSM_WIDTH // SSM_GROUP_CH
SSM_STATE = 64
SSM_LANES = SSM_GROUPS * SSM_STATE
D_FF = 2816
N_MOD = 9
MACARON_WEIGHT = 0.5
DEEPNORM_ALPHA = (2 * DEPTH) ** 0.25
LN_EPS = 1e-5
RMS_EPS = 1e-6
MLA_SCALE = (MLA_NOPE_DIM + MLA_ROPE_DIM) ** -0.5
SB_SCALE = SB_HEAD_DIM ** -0.5

LANE = 128
SUBLANE = 8
VMEM_LIMIT_BYTES = 56 * 1024 * 1024

SB_DEAD_LOG = -104.0
NEG_BIG = -1e30

FFN_CHUNK = 256
MLA_TQ = 256
MLA_TK = 256
SB_TQ = 256
SB_TK = 128
SSM_CHUNK = 512
SSM_PAD = 256
DEC_PAGES = 16


def _cparams(sem):
    return pltpu.CompilerParams(dimension_semantics=sem, vmem_limit_bytes=VMEM_LIMIT_BYTES)


def _dot(a, b):
    return jnp.dot(a, b, preferred_element_type=F32)


def _dot_nt(a, b):
    return lax.dot_general(a, b, (((1,), (1,)), ((), ())), preferred_element_type=F32)


def _rms(x, g):
    return x * lax.rsqrt(jnp.mean(x * x, axis=-1, keepdims=True) + RMS_EPS) * g


def _layer_norm(y, g, b):
    mu = jnp.mean(y, axis=-1, keepdims=True)
    yc = y - mu
    var = jnp.mean(yc * yc, axis=-1, keepdims=True)
    return yc * lax.rsqrt(var + LN_EPS) * g + b


def _ada_kernel(c_ref, w_ref, b_ref, o_ref):
    c = c_ref[...]
    s = (c * jax.nn.sigmoid(c)).astype(BF16)
    o_ref[...] = _dot(s, w_ref[...].astype(BF16)) + b_ref[...]


def _ada(c_all, ada_w, ada_b):
    n = c_all.shape[0]
    depth, _, n_out = ada_w.shape
    tn = D_MODEL
    return pl.pallas_call(
        _ada_kernel,
        grid=(depth, n_out // tn),
        in_specs=[
            pl.BlockSpec((n, D_MODEL), lambda l, j: (0, 0)),
            pl.BlockSpec((None, D_MODEL, tn), lambda l, j: (l, 0, j)),
            pl.BlockSpec((None, 1, tn), lambda l, j: (l, 0, j)),
        ],
        out_specs=pl.BlockSpec((None, n, tn), lambda l, j: (l, 0, j)),
        out_shape=jax.ShapeDtypeStruct((depth, n, n_out), F32),
        compiler_params=_cparams(("parallel", "parallel")),
        name="ada_mod",
    )(c_all, ada_w, ada_b.reshape(depth, 1, n_out))


def _mod_spec(arr, tm, tiles_per_batch):
    if arr.ndim == 3:
        return pl.BlockSpec((None, 1, D_MODEL), lambda i, *_: (i // tiles_per_batch, 0, 0))
    return pl.BlockSpec((tm, D_MODEL), lambda i, *_: (i, 0))


def _ffn_kernel(x_ref, sh_ref, sc_ref, g_ref, wg_ref, wu_ref, wd_ref, lng_ref, lnb_ref, o_ref,
                h_scr, acc_scr):
    j = pl.program_id(1)

    @pl.when(j == 0)
    def _():
        h_scr[...] = (x_ref[...] * (1.0 + sc_ref[...]) + sh_ref[...]).astype(BF16)
        acc_scr[...] = jnp.zeros_like(acc_scr)

    h = h_scr[...]
    gate = _dot(h, wg_ref[...])
    up = _dot(h, wu_ref[...])
    act = (gate * jax.nn.sigmoid(gate) * up).astype(BF16)
    acc_scr[...] += _dot(act, wd_ref[...])

    @pl.when(j == pl.num_programs(1) - 1)
    def _():
        y = DEEPNORM_ALPHA * x_ref[...] + MACARON_WEIGHT * (1.0 + g_ref[...]) * acc_scr[...]
        o_ref[...] = _layer_norm(y, lng_ref[...], lnb_ref[...])


def _ffn(x, sh, sc, g, wg, wu, wd, ln_g3, ln_b3, layer, which, tokens_per_batch):
    n = x.shape[0]
    tm = min(1024, tokens_per_batch)
    tpb = max(tokens_per_batch // tm, 1)
    n_chunks = D_FF // FFN_CHUNK
    ln_idx = layer * 3 + which
    return pl.pallas_call(
        _ffn_kernel,
        grid=(n // tm, n_chunks),
        in_specs=[
            pl.BlockSpec((tm, D_MODEL), lambda i, j: (i, 0)),
            _mod_spec(sh, tm, tpb), _mod_spec(sc, tm, tpb), _mod_spec(g, tm, tpb),
            pl.BlockSpec((None, D_MODEL, FFN_CHUNK), lambda i, j: (layer, 0, j)),
            pl.BlockSpec((None, D_MODEL, FFN_CHUNK), lambda i, j: (layer, 0, j)),
            pl.BlockSpec((None, FFN_CHUNK, D_MODEL), lambda i, j: (layer, j, 0)),
            pl.BlockSpec((None, 1, D_MODEL), lambda i, j: (ln_idx, 0, 0)),
            pl.BlockSpec((None, 1, D_MODEL), lambda i, j: (ln_idx, 0, 0)),
        ],
        out_specs=pl.BlockSpec((tm, D_MODEL), lambda i, j: (i, 0)),
        out_shape=jax.ShapeDtypeStruct((n, D_MODEL), F32),
        scratch_shapes=[pltpu.VMEM((tm, D_MODEL), BF16), pltpu.VMEM((tm, D_MODEL), F32)],
        compiler_params=_cparams(("parallel", "arbitrary")),
        name="ffn_ln",
    )(x, sh, sc, g, wg, wu, wd, ln_g3, ln_b3)


W_IN_EXT = 1536


def _mixer_in_kernel(x_ref, sh_ref, sc_ref, tab_ref, win_ref, qn_ref, kvn_ref, wq_ref, wuk_ref,
                     q_ref, kpad_ref, ckv_ref, kr_ref, sq_ref, sk_ref, sv_ref, skb_ref, svb_ref, u_ref):
    h = (x_ref[...] * (1.0 + sc_ref[...]) + sh_ref[...]).astype(BF16)
    p = _dot(h, win_ref[...])
    tab = tab_ref[...]
    lane = lax.broadcasted_iota(jnp.int32, tab.shape, 1)

    def rope_pair(pair):
        t = pair * tab
        return t + pltpu.roll(t, MLA_ROPE_DIM, 1)

    ckv = _rms(p[:, 256:384], kvn_ref[...])
    kr = rope_pair(p[:, 384:512])
    ckv_ref[...] = ckv
    kr_ref[...] = kr[:, :MLA_ROPE_DIM]
    kpad_ref[:, 0:128] = ckv.astype(BF16)
    kpad_ref[:, 128:256] = jnp.where(lane < MLA_ROPE_DIM, kr, 0.0).astype(BF16)

    sq_ref[...] = (p[:, 512:768] * SB_SCALE).astype(BF16)
    sk = p[:, 768:1024]
    sv = p[:, 1024:1280]
    sk_ref[...] = sk
    sv_ref[...] = sv
    skb_ref[...] = sk.astype(BF16)
    svb_ref[...] = sv.astype(BF16)
    u_ref[...] = p[:, 1280:1536]

    cq = _rms(p[:, 0:256], qn_ref[...]).astype(BF16)
    qa = _dot(cq, wq_ref[...])
    for pr in range(MLA_HEADS // 2):
        lat2 = _dot(qa[:, 256 * pr:256 * (pr + 1)].astype(BF16), wuk_ref[pr])
        for hh in range(2):
            hd = 2 * pr + hh
            q_ref[:, 256 * hd:256 * hd + 128] = lat2[:, 128 * hh:128 * (hh + 1)].astype(BF16)
            q_ref[:, 256 * hd + 128:256 * hd + 256] = rope_pair(
                qa[:, 512 + 128 * hd:640 + 128 * hd]).astype(BF16)


def _mixer_in(x, sh, sc, tab, tab_rows_per_batch, win, qn, kvn, wq, wuk, layer, tokens_per_batch):
    n = x.shape[0]
    tm = min(512, tokens_per_batch)
    tpb = max(tokens_per_batch // tm, 1)
    tab_tiles = tab.shape[0] // tm

    def wspec(shape):
        nd = len(shape)
        return pl.BlockSpec((None,) + shape, lambda i: (layer,) + (0,) * nd)

    def ospec(w):
        return pl.BlockSpec((tm, w), lambda i: (i, 0))

    outs = [(4 * 256, BF16), (256, BF16), (KV_LORA_RANK, F32), (MLA_ROPE_DIM, F32), (256, BF16),
            (256, F32), (256, F32), (256, BF16), (256, BF16), (256, F32)]
    return pl.pallas_call(
        _mixer_in_kernel,
        grid=(n // tm,),
        in_specs=[
            pl.BlockSpec((tm, D_MODEL), lambda i: (i, 0)),
            _mod_spec(sh, tm, tpb), _mod_spec(sc, tm, tpb),
            pl.BlockSpec((tm, LANE), lambda i: (i % tab_tiles, 0)),
            wspec((D_MODEL, W_IN_EXT)), wspec((1, Q_LORA_RANK)), wspec((1, KV_LORA_RANK)),
            wspec((Q_LORA_RANK, 1024)), wspec((2, 256, 256)),
        ],
        out_specs=[ospec(w) for w, _ in outs],
        out_shape=[jax.ShapeDtypeStruct((n, w), dt) for w, dt in outs],
        compiler_params=_cparams(("parallel",)),
        name="mixer_in",
    )(x, sh, sc, tab, win, qn, kvn, wq, wuk)


def _mla_prompt_kernel(q_ref, k_ref, o_ref, m_scr, l_scr, acc_scr):
    qi = pl.program_id(1)
    m_scr[...] = jnp.full_like(m_scr, NEG_BIG)
    l_scr[...] = jnp.zeros_like(l_scr)
    acc_scr[...] = jnp.zeros_like(acc_scr)

    def block(j, masked):
        kblk = k_ref[pl.ds(pl.multiple_of(j * MLA_TK, MLA_TK), MLA_TK), :]
        vblk = kblk[:, :KV_LORA_RANK]
        if masked:
            row = lax.broadcasted_iota(jnp.int32, (MLA_TQ, MLA_TK), 0)
            col = lax.broadcasted_iota(jnp.int32, (MLA_TQ, MLA_TK), 1)
            keep = col <= row
        for hd in range(MLA_HEADS):
            s = _dot_nt(q_ref[:, 256 * hd:256 * (hd + 1)], kblk) * MLA_SCALE
            if masked:
                s = jnp.where(keep, s, NEG_BIG)
            m_prev = m_scr[hd]
            m_new = jnp.maximum(m_prev, jnp.max(s, axis=1, keepdims=True))
            alpha = jnp.exp(m_prev - m_new)
            pexp = jnp.exp(s - m_new)
            l_scr[hd] = alpha * l_scr[hd] + jnp.sum(pexp, axis=1, keepdims=True)
            acc_scr[hd] = alpha * acc_scr[hd] + _dot(pexp.astype(BF16), vblk)
            m_scr[hd] = m_new

    def body(j, carry):
        block(j, False)
        return carry

    lax.fori_loop(0, qi, body, 0)
    block(qi, True)
    for hd in range(MLA_HEADS):
        o_ref[:, 128 * hd:128 * (hd + 1)] = acc_scr[hd] / l_scr[hd]


def _mla_prompt(q, kpad, batch, seq):
    n = q.shape[0]
    nq = seq // MLA_TQ
    return pl.pallas_call(
        _mla_prompt_kernel,
        grid=(batch, nq),
        in_specs=[
            pl.BlockSpec((MLA_TQ, 1024), lambda b, i: (b * nq + i, 0)),
            pl.BlockSpec((seq, 256), lambda b, i: (b, 0)),
        ],
        out_specs=pl.BlockSpec((MLA_TQ, MLA_HEADS * KV_LORA_RANK), lambda b, i: (b * nq + i, 0)),
        out_shape=jax.ShapeDtypeStruct((n, MLA_HEADS * KV_LORA_RANK), F32),
        scratch_shapes=[pltpu.VMEM((MLA_HEADS, MLA_TQ, 1), F32), pltpu.VMEM((MLA_HEADS, MLA_TQ, 1), F32),
                        pltpu.VMEM((MLA_HEADS, MLA_TQ, KV_LORA_RANK), F32)],
        compiler_params=_cparams(("parallel", "parallel")),
        name="mla_prompt",
    )(q, kpad)


def _lane_head(shape):
    return lax.shift_right_logical(lax.broadcasted_iota(jnp.int32, shape, 1), int(math.log2(SB_HEAD_DIM)))


def _suffix_ones(tk):
    r = lax.broadcasted_iota(jnp.int32, (tk, tk), 0)
    c = lax.broadcasted_iota(jnp.int32, (tk, tk), 1)
    return jnp.where(r > c, 1.0, 0.0).astype(BF16)


def _sb_logs(z):
    sp = jnp.log1p(jnp.exp(-jnp.abs(z)))
    log_beta = jnp.minimum(z, 0.0) - sp
    log_keep = -jnp.maximum(z, 0.0) - sp
    return log_beta, log_keep


def _sb_block(z, carry, umat, allowed):
    log_beta, lk = _sb_logs(z)
    if allowed is not None:
        lk = jnp.where(allowed, lk, 0.0)
    hi = lk.astype(BF16)
    lo = (lk - hi.astype(F32)).astype(BF16)
    excl = _dot(hi, umat) + _dot(lo, umat)
    w = jnp.exp(log_beta + carry + excl)
    if allowed is not None:
        w = jnp.where(allowed, w, 0.0)
    return w, carry + excl[:, 0:1] + lk[:, 0:1]


def _sb_prompt_kernel(q_ref, k_ref, v_ref, o_ref, qm_scr, c_scr, acc_scr):
    qi = pl.program_id(1)
    lane_head = _lane_head((SB_TQ, SB_WIDTH))
    q = q_ref[...]
    for hd in range(SB_HEADS):
        qm_scr[hd] = jnp.where(lane_head == hd, q, jnp.zeros_like(q))
    c_scr[...] = jnp.zeros_like(c_scr)
    acc_scr[...] = jnp.zeros_like(acc_scr)
    umat = _suffix_ones(SB_TK)

    def block(j, masked):
        start = pl.multiple_of(j * SB_TK, SB_TK)
        kblk = k_ref[pl.ds(start, SB_TK), :]
        vblk = v_ref[pl.ds(start, SB_TK), :]
        allowed = None
        if masked:
            row = lax.broadcasted_iota(jnp.int32, (SB_TQ, SB_TK), 0) + qi * SB_TQ
            col = lax.broadcasted_iota(jnp.int32, (SB_TQ, SB_TK), 1) + j * SB_TK
            allowed = col < row
        acc = acc_scr[...]
        for hd in range(SB_HEADS):
            z = _dot_nt(qm_scr[hd], kblk)
            w, c_new = _sb_block(z, c_scr[hd], umat, allowed)
            c_scr[hd] = c_new
            acc = acc + jnp.where(lane_head == hd, _dot(w.astype(BF16), vblk), 0.0)
        acc_scr[...] = acc

    n_diag = SB_TQ // SB_TK
    j_top = (qi + 1) * n_diag - 1
    for d in range(n_diag):
        block(j_top - d, True)

    def alive():
        c_max = functools.reduce(jnp.maximum, [c_scr[hd] for hd in range(SB_HEADS)])
        return (jnp.max(c_max) >= SB_DEAD_LOG).astype(jnp.int32)

    def cond(carry):
        j, live = carry
        return jnp.logical_and(j >= 0, live > 0)

    def body(carry):
        j, _ = carry
        block(j, False)
        return j - 1, alive()

    lax.while_loop(cond, body, (qi * n_diag - 1, alive()))
    o_ref[...] = acc_scr[...]


def _sb_prompt(sq, skb, svb, batch, seq):
    n = sq.shape[0]
    nq = seq // SB_TQ
    return pl.pallas_call(
        _sb_prompt_kernel,
        grid=(batch, nq),
        in_specs=[
            pl.BlockSpec((SB_TQ, SB_WIDTH), lambda b, i: (b * nq + i, 0)),
            pl.BlockSpec((seq, SB_WIDTH), lambda b, i: (b, 0)),
            pl.BlockSpec((seq, SB_WIDTH), lambda b, i: (b, 0)),
        ],
        out_specs=pl.BlockSpec((SB_TQ, SB_WIDTH), lambda b, i: (b * nq + i, 0)),
        out_shape=jax.ShapeDtypeStruct((n, SB_WIDTH), F32),
        scratch_shapes=[pltpu.VMEM((SB_HEADS, SB_TQ, SB_WIDTH), BF16), pltpu.VMEM((SB_HEADS, SB_TQ, 1), F32),
                        pltpu.VMEM((SB_TQ, SB_WIDTH), F32)],
        compiler_params=_cparams(("parallel", "parallel")),
        name="sb_prompt",
    )(sq, skb, svb)


def _ssm_prep_kernel(lr_ref, li_ref, ldt_ref, lrc_ref, lic_ref, ldtc_ref, bre_ref, bim_ref,
                     pw_ref, bbre_ref, bbim_ref):
    def discretise(lr, li, ldt):
        dt = jnp.exp(ldt)
        mag = jnp.exp(lr * dt)
        a_re = mag * jnp.cos(li * dt)
        a_im = mag * jnp.sin(li * dt)
        den = lr * lr + li * li
        z_re = ((a_re - 1.0) * lr + a_im * li) / den
        z_im = (a_im * lr - (a_re - 1.0) * li) / den
        return a_re, a_im, z_re, z_im

    a_re, a_im, _, _ = discretise(lr_ref[...], li_ref[...], ldt_ref[...])
    for k in range(pw_ref.shape[0]):
        pw_ref[k:k + 1, 0:SSM_LANES] = a_re
        pw_ref[k:k + 1, SSM_LANES:2 * SSM_LANES] = a_im
        a_re, a_im = a_re * a_re - a_im * a_im, 2.0 * a_re * a_im
    _, _, z_re, z_im = discretise(lrc_ref[...], lic_ref[...], ldtc_ref[...])
    bre, bim = bre_ref[...], bim_ref[...]
    bbre_ref[...] = z_re * bre - z_im * bim
    bbim_ref[...] = z_re * bim + z_im * bre


def _ssm_prep(lam_re, lam_im, log_dt, b_re, b_im, n_pw):
    depth = lam_re.shape[0]
    ldt = jnp.repeat(log_dt, SSM_STATE, axis=1)
    rows = [a.reshape(depth, 1, SSM_LANES) for a in (lam_re, lam_im, ldt)]
    cols = [a.reshape(depth, SSM_LANES, 1) for a in (lam_re, lam_im, ldt)]
    bs = [a.reshape(depth, SSM_LANES, SSM_GROUP_CH) for a in (b_re, b_im)]
    row_spec = pl.BlockSpec((None, 1, SSM_LANES), lambda l: (l, 0, 0))
    col_spec = pl.BlockSpec((None, SSM_LANES, 1), lambda l: (l, 0, 0))
    b_spec = pl.BlockSpec((None, SSM_LANES, SSM_GROUP_CH), lambda l: (l, 0, 0))
    return pl.pallas_call(
        _ssm_prep_kernel,
        grid=(depth,),
        in_specs=[row_spec] * 3 + [col_spec] * 3 + [b_spec] * 2,
        out_specs=[pl.BlockSpec((None, n_pw, 2 * SSM_LANES), lambda l: (l, 0, 0)), b_spec, b_spec],
        out_shape=[jax.ShapeDtypeStruct((depth, n_pw, 2 * SSM_LANES), F32),
                   jax.ShapeDtypeStruct((depth, SSM_LANES, SSM_GROUP_CH), F32),
                   jax.ShapeDtypeStruct((depth, SSM_LANES, SSM_GROUP_CH), F32)],
        compiler_params=_cparams(("parallel",)),
        name="ssm_prep",
    )(*rows, *cols, *bs)


def _ssm_scan_kernel(seg_len, chunks_per_seg, has_h0, *refs):
    if has_h0:
        u_ref, h0_ref, wb_ref, wc_ref, pw_ref, d_ref, y_ref, h_ref, buf_a, buf_b, carry_scr = refs
    else:
        u_ref, wb_ref, wc_ref, pw_ref, d_ref, y_ref, h_ref, buf_a, buf_b, carry_scr = refs
    i = pl.program_id(0)
    rows = u_ref.shape[0]
    n_tiles = rows // SUBLANE
    width = 2 * SSM_LANES

    @pl.when(i == 0)
    def _():
        buf_a[0:SSM_PAD, :] = jnp.zeros((SSM_PAD, width), F32)
        buf_b[0:SSM_PAD, :] = jnp.zeros((SSM_PAD, width), F32)
        carry_scr[...] = jnp.zeros_like(carry_scr)

    def cmul_add(base_re, base_im, a_re, a_im, v_re, v_im):
        return (base_re + a_re * v_re - a_im * v_im, base_im + a_re * v_im + a_im * v_re)

    u = u_ref[...]
    x = _dot(u.astype(BF16), wb_ref[...])
    a_re = pw_ref[0:1, 0:SSM_LANES]
    a_im = pw_ref[0:1, SSM_LANES:width]
    if has_h0:
        h0 = h0_ref[...]
        x_re, x_im = cmul_add(x[:, :SSM_LANES], x[:, SSM_LANES:], a_re, a_im,
                              h0[:, :SSM_LANES], h0[:, SSM_LANES:])
        buf_a[SSM_PAD:SSM_PAD + rows, 0:SSM_LANES] = x_re
        buf_a[SSM_PAD:SSM_PAD + rows, SSM_LANES:width] = x_im
    else:
        buf_a[SSM_PAD:SSM_PAD + rows, :] = x
        first = (i % chunks_per_seg) == 0
        cr = jnp.where(first, 0.0, carry_scr[:, 0:SSM_LANES])
        ci = jnp.where(first, 0.0, carry_scr[:, SSM_LANES:width])
        r0_re, r0_im = cmul_add(x[0:1, :SSM_LANES], x[0:1, SSM_LANES:], a_re, a_im, cr, ci)
        buf_a[SSM_PAD:SSM_PAD + 1, 0:SSM_LANES] = r0_re
        buf_a[SSM_PAD:SSM_PAD + 1, SSM_LANES:width] = r0_im

    span = min(seg_len, rows)
    shifts = [1 << k for k in range(int(math.log2(span)))]
    src, dst = buf_a, buf_b
    row_in_tile = lax.broadcasted_iota(jnp.int32, (SUBLANE, SSM_LANES), 0)
    for k, d in enumerate(shifts):
        p_re = jnp.broadcast_to(pw_ref[k:k + 1, 0:SSM_LANES], (SUBLANE, SSM_LANES))
        p_im = jnp.broadcast_to(pw_ref[k:k + 1, SSM_LANES:width], (SUBLANE, SSM_LANES))

        def tile_step(r, carry, src=src, dst=dst, d=d, p_re=p_re, p_im=p_im):
            base = pl.multiple_of(SSM_PAD + r * SUBLANE, SUBLANE)
            cur_re = src[pl.ds(base, SUBLANE), 0:SSM_LANES]
            cur_im = src[pl.ds(base, SUBLANE), SSM_LANES:width]
            if d >= SUBLANE:
                sbase = pl.multiple_of(SSM_PAD + r * SUBLANE - d, SUBLANE)
                sh_re = src[pl.ds(sbase, SUBLANE), 0:SSM_LANES]
                sh_im = src[pl.ds(sbase, SUBLANE), SSM_LANES:width]
            elif seg_len < SUBLANE:
                keep = (row_in_tile & (seg_len - 1)) >= d
                sh_re = jnp.where(keep, pltpu.roll(cur_re, d, 0), 0.0)
                sh_im = jnp.where(keep, pltpu.roll(cur_im, d, 0), 0.0)
            else:
                pbase = pl.multiple_of(SSM_PAD + r * SUBLANE - SUBLANE, SUBLANE)
                prev_re = src[pl.ds(pbase, SUBLANE), 0:SSM_LANES]
                prev_im = src[pl.ds(pbase, SUBLANE), SSM_LANES:width]
                own = row_in_tile >= d
                sh_re = jnp.where(own, pltpu.roll(cur_re, d, 0), pltpu.roll(prev_re, d, 0))
                sh_im = jnp.where(own, pltpu.roll(cur_im, d, 0), pltpu.roll(prev_im, d, 0))
            new_re, new_im = cmul_add(cur_re, cur_im, p_re, p_im, sh_re, sh_im)
            dst[pl.ds(base, SUBLANE), 0:SSM_LANES] = new_re
            dst[pl.ds(base, SUBLANE), SSM_LANES:width] = new_im
            return carry

        lax.fori_loop(0, n_tiles, tile_step, 0)
        src, dst = dst, src

    y_ref[...] = _dot(src[SSM_PAD:SSM_PAD + rows, :].astype(BF16), wc_ref[...]) + d_ref[...] * u
    if has_h0:
        h_ref[...] = src[SSM_PAD:SSM_PAD + rows, :]
    else:
        last_row = src[SSM_PAD + rows - 1:SSM_PAD + rows, :]
        carry_scr[...] = last_row
        h_ref[...] = last_row


def _ssm_scan(u, h0_rows, wb, wc, pw, d_row, layer, seg_len):
    n = u.shape[0]
    rows = min(SSM_CHUNK, n)
    n_chunks = n // rows
    has_h0 = h0_rows is not None
    chunks_per_seg = max(seg_len // rows, 1)
    width = 2 * SSM_LANES
    n_pw = pw.shape[1]
    in_specs = [pl.BlockSpec((rows, SSM_WIDTH), lambda i: (i, 0))]
    args = [u]
    if has_h0:
        in_specs.append(pl.BlockSpec((rows, width), lambda i: (i, 0)))
        args.append(h0_rows)
        h_spec = pl.BlockSpec((rows, width), lambda i: (i, 0))
        h_shape = jax.ShapeDtypeStruct((n, width), F32)
    else:
        h_spec = pl.BlockSpec((None, 1, width), lambda i: (i, 0, 0))
        h_shape = jax.ShapeDtypeStruct((n_chunks, 1, width), F32)
    in_specs += [
        pl.BlockSpec((None, SSM_WIDTH, width), lambda i: (layer, 0, 0)),
        pl.BlockSpec((None, width, SSM_WIDTH), lambda i: (layer, 0, 0)),
        pl.BlockSpec((None, n_pw, width), lambda i: (layer, 0, 0)),
        pl.BlockSpec((None, 1, SSM_WIDTH), lambda i: (layer, 0, 0)),
    ]
    args += [wb, wc, pw, d_row]
    return pl.pallas_call(
        functools.partial(_ssm_scan_kernel, seg_len, chunks_per_seg, has_h0),
        grid=(n_chunks,),
        in_specs=in_specs,
        out_specs=[pl.BlockSpec((rows, SSM_WIDTH), lambda i: (i, 0)), h_spec],
        out_shape=[jax.ShapeDtypeStruct((n, SSM_WIDTH), F32), h_shape],
        scratch_shapes=[pltpu.VMEM((SSM_PAD + rows, width), F32), pltpu.VMEM((SSM_PAD + rows, width), F32),
                        pltpu.VMEM((1, width), F32)],
        compiler_params=_cparams(("arbitrary",)),
        name="ssm_scan",
    )(*args)


def _mixer_out_kernel(x_ref, g_ref, olat_ref, osb_ref, y_ref, wuv_ref, wglu_ref, bglu_ref,
                      nm_ref, ns_ref, nz_ref, wout_ref, lng_ref, lnb_ref, o_ref):
    olat = olat_ref[...].astype(BF16)
    o_mla = jnp.concatenate([_dot(olat[:, 0:256], wuv_ref[0]), _dot(olat[:, 256:512], wuv_ref[1])], axis=1)
    yg = jax.nn.gelu(y_ref[...], approximate=True)
    o_ssm = yg * jax.nn.sigmoid(_dot(yg.astype(BF16), wglu_ref[...]) + bglu_ref[...])
    merged = jnp.concatenate([_rms(o_mla, nm_ref[...]), _rms(osb_ref[...], ns_ref[...]),
                              _rms(o_ssm, nz_ref[...])], axis=1).astype(BF16)
    m = _dot(merged, wout_ref[...])
    y = DEEPNORM_ALPHA * x_ref[...] + (1.0 + g_ref[...]) * m
    o_ref[...] = _layer_norm(y, lng_ref[...], lnb_ref[...])


def _mixer_out(x, g, olat, osb, y, wuv, wglu, bglu, nm, ns, nz, wout, ln_g3, ln_b3, layer, tokens_per_batch):
    n = x.shape[0]
    tm = min(512, tokens_per_batch)
    tpb = max(tokens_per_batch // tm, 1)
    ln_idx = layer * 3 + 1

    def wspec(shape):
        nd = len(shape)
        return pl.BlockSpec((None,) + shape, lambda i: (layer,) + (0,) * nd)

    return pl.pallas_call(
        _mixer_out_kernel,
        grid=(n // tm,),
        in_specs=[
            pl.BlockSpec((tm, D_MODEL), lambda i: (i, 0)),
            _mod_spec(g, tm, tpb),
            pl.BlockSpec((tm, 512), lambda i: (i, 0)),
            pl.BlockSpec((tm, SB_WIDTH), lambda i: (i, 0)),
            pl.BlockSpec((tm, SSM_WIDTH), lambda i: (i, 0)),
            wspec((2, 256, 256)), wspec((SSM_WIDTH, SSM_WIDTH)), wspec((1, SSM_WIDTH)),
            wspec((1, 512)), wspec((1, SB_WIDTH)), wspec((1, SSM_WIDTH)),
            wspec((D_MODEL, D_MODEL)),
            pl.BlockSpec((None, 1, D_MODEL), lambda i: (ln_idx, 0, 0)),
            pl.BlockSpec((None, 1, D_MODEL), lambda i: (ln_idx, 0, 0)),
        ],
        out_specs=pl.BlockSpec((tm, D_MODEL), lambda i: (i, 0)),
        out_shape=jax.ShapeDtypeStruct((n, D_MODEL), F32),
        compiler_params=_cparams(("parallel",)),
        name="mixer_out",
    )(x, g, olat, osb, y, wuv, wglu, bglu, nm, ns, nz, wout, ln_g3, ln_b3)


def _mla_decode_kernel(n_pg, pt_ref, q_ref, knew_ref, *refs):
    ckv_refs = refs[0:n_pg]
    kr_refs = refs[n_pg:2 * n_pg]
    o_ref, m_scr, l_scr, acc_scr = refs[2 * n_pg:]
    c = pl.program_id(1)
    rows = q_ref.shape[0]
    q = q_ref[...]

    @pl.when(c == 0)
    def _():
        qf = q.astype(F32)
        knew = knew_ref[...].astype(F32)
        tok = lax.shift_right_logical(lax.broadcasted_iota(jnp.int32, (rows, 1), 0), 2)
        s_new = [jnp.where(tok >= s, jnp.sum(qf * knew[s:s + 1, :], axis=1, keepdims=True) * MLA_SCALE, NEG_BIG)
                 for s in range(knew.shape[0])]
        m = functools.reduce(jnp.maximum, s_new)
        l = jnp.zeros((rows, 1), F32)
        acc = jnp.zeros((rows, KV_LORA_RANK), F32)
        for s in range(knew.shape[0]):
            pexp = jnp.exp(s_new[s] - m)
            l = l + pexp
            acc = acc + pexp.astype(BF16).astype(F32) * knew[s:s + 1, 0:KV_LORA_RANK]
        m_scr[...] = m
        l_scr[...] = l
        acc_scr[...] = acc

    q_lat = q[:, 0:KV_LORA_RANK]
    q_rope = q[:, KV_LORA_RANK:KV_LORA_RANK + MLA_ROPE_DIM]
    ckv = [r[...].astype(BF16) for r in ckv_refs]
    s = jnp.concatenate(
        [_dot_nt(q_lat, ckv[j]) + _dot(q_rope, kr_refs[j][...].astype(BF16)) for j in range(n_pg)],
        axis=1) * MLA_SCALE
    m_prev = m_scr[...]
    m_new = jnp.maximum(m_prev, jnp.max(s, axis=1, keepdims=True))
    alpha = jnp.exp(m_prev - m_new)
    pexp = jnp.exp(s - m_new)
    l_scr[...] = alpha * l_scr[...] + jnp.sum(pexp, axis=1, keepdims=True)
    pb = pexp.astype(BF16)
    pv = _dot(pb[:, 0:PAGE_SIZE], ckv[0])
    for j in range(1, n_pg):
        pv = pv + _dot(pb[:, j * PAGE_SIZE:(j + 1) * PAGE_SIZE], ckv[j])
    acc_scr[...] = alpha * acc_scr[...] + pv
    m_scr[...] = m_new

    @pl.when(c == pl.num_programs(1) - 1)
    def _():
        o_ref[...] = acc_scr[...] / l_scr[...]


def _mla_decode(page_table_flat, q16, knew, cache_ckv, cache_krope_t, layer, n_pages):
    nb, rows, _ = q16.shape
    n_pg = min(DEC_PAGES, n_pages)
    n_steps = n_pages // n_pg

    def page_spec(page_shape, j):
        return pl.BlockSpec((None, None) + page_shape,
                            lambda b, c, pt: (pt[b * n_pages + c * n_pg + j], layer, 0, 0))

    grid_spec = pltpu.PrefetchScalarGridSpec(
        num_scalar_prefetch=1,
        grid=(nb, n_steps),
        in_specs=[pl.BlockSpec((None, rows, 256), lambda b, c, pt: (b, 0, 0)),
                  pl.BlockSpec((None, knew.shape[1], 256), lambda b, c, pt: (b, 0, 0))]
        + [page_spec((PAGE_SIZE, KV_LORA_RANK), j) for j in range(n_pg)]
        + [page_spec((MLA_ROPE_DIM, PAGE_SIZE), j) for j in range(n_pg)],
        out_specs=pl.BlockSpec((None, rows, KV_LORA_RANK), lambda b, c, pt: (b, 0, 0)),
        scratch_shapes=[pltpu.VMEM((rows, 1), F32), pltpu.VMEM((rows, 1), F32),
                        pltpu.VMEM((rows, KV_LORA_RANK), F32)],
    )
    return pl.pallas_call(
        functools.partial(_mla_decode_kernel, n_pg),
        grid_spec=grid_spec,
        out_shape=jax.ShapeDtypeStruct((nb, rows, KV_LORA_RANK), F32),
        compiler_params=_cparams(("parallel", "arbitrary")),
        name="mla_decode",
    )(page_table_flat, q16, knew, *([cache_ckv] * n_pg), *([cache_krope_t] * n_pg))


SB_DEC_ROWS = SB_HEADS * SUBLANE


def _sb_decode_kernel(n_pages, layer, pt_ref, q_ref, knew_ref, vnew_ref, k_hbm, v_hbm, o_ref,
                      kbuf, vbuf, sem, c_scr, acc_scr):
    b = pl.program_id(0)

    def copies(p, slot):
        pid = pt_ref[b * n_pages + p]
        return (pltpu.make_async_copy(k_hbm.at[pid, layer], kbuf.at[slot], sem.at[0, slot]),
                pltpu.make_async_copy(v_hbm.at[pid, layer], vbuf.at[slot], sem.at[1, slot]))

    def start(p, slot):
        for cp in copies(p, slot):
            cp.start()

    def wait(p, slot):
        for cp in copies(p, slot):
            cp.wait()

    start(n_pages - 1, 0)

    q = q_ref[...]
    qf = q.astype(F32)
    tok = lax.broadcasted_iota(jnp.int32, (SB_DEC_ROWS, 1), 0) & (SUBLANE - 1)
    n_new = knew_ref.shape[0]
    knew = knew_ref[...].astype(F32)
    vnew = vnew_ref[...].astype(F32)
    carry = jnp.where(tok < n_new, 0.0, NEG_BIG)
    acc = jnp.zeros((SB_DEC_ROWS, SB_WIDTH), F32)
    for s in range(n_new - 1, -1, -1):
        z = jnp.sum(qf * knew[s:s + 1, :], axis=1, keepdims=True)
        log_beta, lk = _sb_logs(z)
        allowed = tok > s
        w = jnp.where(allowed, jnp.exp(log_beta + carry), 0.0)
        carry = carry + jnp.where(allowed, lk, 0.0)
        acc = acc + w.astype(BF16).astype(F32) * vnew[s:s + 1, :]
    c_scr[...] = carry
    acc_scr[...] = acc
    umat = _suffix_ones(PAGE_SIZE)

    def alive():
        return (jnp.max(c_scr[...]) >= SB_DEAD_LOG).astype(jnp.int32)

    def cond(state):
        p, live = state
        return jnp.logical_and(p >= 0, live > 0)

    def body(state):
        p, _ = state
        slot = (n_pages - 1 - p) & 1
        wait(p, slot)

        @pl.when(p > 0)
        def _():
            start(p - 1, 1 - slot)

        kpage_t = kbuf[slot].astype(BF16)
        vpage_t = vbuf[slot].astype(BF16)
        z = _dot(q, kpage_t)
        w, c_new = _sb_block(z, c_scr[...], umat, None)
        c_scr[...] = c_new
        acc_scr[...] += _dot_nt(w.astype(BF16), vpage_t)
        return p - 1, alive()

    p_end, _ = lax.while_loop(cond, body, (n_pages - 1, alive()))

    @pl.when(p_end >= 0)
    def _():
        wait(p_end, (n_pages - 1 - p_end) & 1)

    lane_head = _lane_head((SUBLANE, SB_WIDTH))
    out = jnp.zeros((SUBLANE, SB_WIDTH), F32)
    for hd in range(SB_HEADS):
        out = out + jnp.where(lane_head == hd, acc_scr[hd * SUBLANE:(hd + 1) * SUBLANE, :], 0.0)
    o_ref[...] = out


def _sb_decode(page_table_flat, q32, knew, vnew, cache_k, cache_v, layer, n_pages):
    nb = q32.shape[0]
    n_new = knew.shape[1]
    grid_spec = pltpu.PrefetchScalarGridSpec(
        num_scalar_prefetch=1,
        grid=(nb,),
        in_specs=[pl.BlockSpec((None, SB_DEC_ROWS, SB_WIDTH), lambda b, pt: (b, 0, 0)),
                  pl.BlockSpec((None, n_new, SB_WIDTH), lambda b, pt: (b, 0, 0)),
                  pl.BlockSpec((None, n_new, SB_WIDTH), lambda b, pt: (b, 0, 0)),
                  pl.BlockSpec(memory_space=pl.ANY),
                  pl.BlockSpec(memory_space=pl.ANY)],
        out_specs=pl.BlockSpec((None, SUBLANE, SB_WIDTH), lambda b, pt: (b, 0, 0)),
        scratch_shapes=[pltpu.VMEM((2, SB_WIDTH, PAGE_SIZE), F32), pltpu.VMEM((2, SB_WIDTH, PAGE_SIZE), F32),
                        pltpu.SemaphoreType.DMA((2, 2)),
                        pltpu.VMEM((SB_DEC_ROWS, 1), F32), pltpu.VMEM((SB_DEC_ROWS, SB_WIDTH), F32)],
    )
    return pl.pallas_call(
        functools.partial(_sb_decode_kernel, n_pages, layer),
        grid_spec=grid_spec,
        out_shape=jax.ShapeDtypeStruct((nb, SUBLANE, SB_WIDTH), F32),
        compiler_params=_cparams(("arbitrary",)),
        name="sb_decode",
    )(page_table_flat, q32, knew, vnew, cache_k, cache_v)


def _swap_halves(w):
    half = w.shape[-1] // 2
    return jnp.concatenate([w[..., half:], w[..., :half]], axis=-1)


def _block_diag_pairs(w):
    depth, c, heads, d = w.shape
    wt = jnp.transpose(w, (0, 2, 3, 1)).reshape(depth, heads // 2, 2, d, c)
    eye = jnp.eye(2, dtype=w.dtype)
    return jnp.einsum('lpadc,ab->lpadbc', wt, eye).reshape(depth, heads // 2, 2 * d, 2 * c)


def _rope_table(pos):
    half = MLA_ROPE_DIM // 2
    freqs = ROPE_BASE ** (-jnp.arange(half, dtype=F32) / half)
    ang = pos.astype(F32)[:, None] * freqs[None, :]
    cos, sin = jnp.cos(ang), jnp.sin(ang)
    return jnp.concatenate([cos, cos, -sin, sin], axis=-1)


def kernel(x_prompt, x_sample, c_prompt, c_sample, page_table, cache_ckv, cache_krope, cache_sb_k, cache_sb_v, state_ssm_re, state_ssm_im, ada_w, ada_b, ffn1_w_gate, ffn1_w_up, ffn1_w_down, ffn2_w_gate, ffn2_w_up, ffn2_w_down, ln_g, ln_b, w_in, mla_q_norm, mla_kv_norm, mla_w_uq, mla_w_uk, mla_w_uv, ssm_lam_re, ssm_lam_im, ssm_log_dt, ssm_b_re, ssm_b_im, ssm_c_re, ssm_c_im, ssm_d, ssm_w_glu, ssm_b_glu, out_norm_mla, out_norm_sb, out_norm_ssm, w_out):
    batch, seq, _ = x_prompt.shape
    dec_batch, dec_seq, _ = x_sample.shape
    n_pages = page_table.shape[1]
    past_len = n_pages * PAGE_SIZE
    n_p, n_s = batch * seq, dec_batch * dec_seq
    depth = ada_w.shape[0]
    assert seq % SSM_CHUNK == 0 and seq % MLA_TQ == 0 and seq % SB_TQ == 0
    assert n_pages % min(DEC_PAGES, n_pages) == 0 and dec_seq < SUBLANE and (dec_seq & (dec_seq - 1)) == 0
    assert n_s <= SSM_CHUNK and n_s % SUBLANE == 0

    bf = lambda a: a.astype(BF16)
    ffn_w = [(bf(ffn1_w_gate), bf(ffn1_w_up), bf(ffn1_w_down)), (bf(ffn2_w_gate), bf(ffn2_w_up), bf(ffn2_w_down))]
    kr_cols = w_in[:, :, 384:448]
    win_ext = bf(jnp.concatenate([w_in[:, :, 0:448], _swap_halves(kr_cols), w_in[:, :, 448:]], axis=-1))
    nope = mla_w_uq[..., :MLA_NOPE_DIM].reshape(depth, Q_LORA_RANK, MLA_HEADS * MLA_NOPE_DIM)
    rope_w = mla_w_uq[..., MLA_NOPE_DIM:]
    rope_pairs = jnp.concatenate([rope_w, _swap_halves(rope_w)], axis=-1).reshape(depth, Q_LORA_RANK, MLA_HEADS * 128)
    wq = bf(jnp.concatenate([nope, rope_pairs], axis=-1))
    wuk_bd = bf(_block_diag_pairs(mla_w_uk))
    wuv_bd = bf(_block_diag_pairs(jnp.transpose(mla_w_uv, (0, 3, 2, 1))))
    wglu = bf(ssm_w_glu)
    wout = bf(w_out)
    row = lambda a: a.reshape(depth, 1, a.shape[-1])
    qn, kvn = row(mla_q_norm), row(mla_kv_norm)
    nm, ns, nz, bglu = row(out_norm_mla), row(out_norm_sb), row(out_norm_ssm), row(ssm_b_glu)
    d_row = ssm_d.reshape(depth, 1, SSM_WIDTH)
    ln_g3 = ln_g.reshape(depth * 3, 1, D_MODEL)
    ln_b3 = ln_b.reshape(depth * 3, 1, D_MODEL)

    n_pw = max(int(math.log2(SSM_CHUNK)), 1)
    pw, bb_re, bb_im = _ssm_prep(ssm_lam_re, ssm_lam_im, ssm_log_dt, ssm_b_re, ssm_b_im, n_pw)
    eye_g = jnp.eye(SSM_GROUPS, dtype=F32)

    def b_map(bb):
        bb = bb.reshape(depth, SSM_GROUPS, SSM_STATE, SSM_GROUP_CH)
        return jnp.einsum('lgph,gk->lghkp', bb, eye_g).reshape(depth, SSM_WIDTH, SSM_LANES)

    def c_map(cc):
        return jnp.einsum('lghp,gk->lgpkh', cc, eye_g).reshape(depth, SSM_LANES, SSM_WIDTH)

    wb = bf(jnp.concatenate([b_map(bb_re), b_map(bb_im)], axis=-1))
    wc = bf(jnp.concatenate([c_map(ssm_c_re), -c_map(ssm_c_im)], axis=1))

    mod = _ada(jnp.concatenate([c_prompt, c_sample], axis=0), ada_w, ada_b)
    tab_p = _rope_table(jnp.arange(seq))
    tab_s = jnp.tile(_rope_table(past_len + jnp.arange(dec_seq)), (dec_batch, 1))
    pt_flat = page_table.reshape(-1).astype(jnp.int32)
    cache_krope_t = jnp.swapaxes(cache_krope, 2, 3)
    sb_page_t = lambda c: jnp.transpose(c, (0, 1, 3, 4, 2)).reshape(c.shape[:2] + (SB_WIDTH, PAGE_SIZE))
    cache_k4, cache_v4 = sb_page_t(cache_sb_k), sb_page_t(cache_sb_v)
    head_mask = (jnp.arange(SB_WIDTH)[None, :] // SB_HEAD_DIM == jnp.arange(SB_HEADS)[:, None])

    xp = x_prompt.reshape(n_p, D_MODEL)
    xs = x_sample.reshape(n_s, D_MODEL)
    st_p, st_s = [], []
    for l in range(depth):
        mp = mod[l, :batch].reshape(batch, N_MOD, 1, D_MODEL)
        ms = jnp.repeat(mod[l, batch:], dec_seq, axis=0).reshape(n_s, N_MOD, D_MODEL)
        mod_p = [mp[:, k] for k in range(N_MOD)]
        mod_s = [ms[:, k] for k in range(N_MOD)]

        xp = _ffn(xp, mod_p[0], mod_p[1], mod_p[2], *ffn_w[0], ln_g3, ln_b3, l, 0, seq)
        q, kpad, ckv, kr, sq, sk, sv, skb, svb, u = _mixer_in(
            xp, mod_p[3], mod_p[4], tab_p, seq, win_ext, qn, kvn, wq, wuk_bd, l, seq)
        olat = _mla_prompt(q, kpad, batch, seq)
        osb = _sb_prompt(sq, skb, svb, batch, seq)
        y, h_last = _ssm_scan(u, None, wb, wc, pw, d_row, l, seq)
        xp = _mixer_out(xp, mod_p[5], olat, osb, y, wuv_bd, wglu, bglu, nm, ns, nz, wout, ln_g3, ln_b3, l, seq)
        xp = _ffn(xp, mod_p[6], mod_p[7], mod_p[8], *ffn_w[1], ln_g3, ln_b3, l, 2, seq)
        cps = seq // SSM_CHUNK
        h_fin = h_last.reshape(batch, cps, 2 * SSM_LANES)[:, cps - 1]
        st_p.append((ckv.reshape(batch, seq, KV_LORA_RANK), kr.reshape(batch, seq, MLA_ROPE_DIM),
                     sk.reshape(batch, seq, SB_HEADS, SB_HEAD_DIM), sv.reshape(batch, seq, SB_HEADS, SB_HEAD_DIM),
                     h_fin[:, :SSM_LANES].reshape(batch, SSM_GROUPS, SSM_STATE),
                     h_fin[:, SSM_LANES:].reshape(batch, SSM_GROUPS, SSM_STATE)))

        xs = _ffn(xs, mod_s[0], mod_s[1], mod_s[2], *ffn_w[0], ln_g3, ln_b3, l, 0, n_s)
        q, kpad, ckv, kr, sq, sk, sv, skb, svb, u = _mixer_in(
            xs, mod_s[3], mod_s[4], tab_s, n_s, win_ext, qn, kvn, wq, wuk_bd, l, n_s)
        q16 = q.reshape(dec_batch, dec_seq * MLA_HEADS, 256)
        olat = _mla_decode(pt_flat, q16, kpad.reshape(dec_batch, dec_seq, 256), cache_ckv, cache_krope_t, l, n_pages)
        olat = olat.reshape(n_s, MLA_HEADS * KV_LORA_RANK)
        sq3 = sq.reshape(dec_batch, 1, dec_seq, SB_WIDTH)
        q32 = jnp.where(head_mask[None, :, None, :], sq3, jnp.zeros_like(sq3))
        q32 = jnp.pad(q32, ((0, 0), (0, 0), (0, SUBLANE - dec_seq), (0, 0))).reshape(dec_batch, SB_DEC_ROWS, SB_WIDTH)
        osb = _sb_decode(pt_flat, q32, skb.reshape(dec_batch, dec_seq, SB_WIDTH),
                         svb.reshape(dec_batch, dec_seq, SB_WIDTH), cache_k4, cache_v4, l, n_pages)
        osb = osb[:, :dec_seq].reshape(n_s, SB_WIDTH)
        h0 = jnp.concatenate([state_ssm_re[:, l].reshape(dec_batch, 1, SSM_LANES),
                              state_ssm_im[:, l].reshape(dec_batch, 1, SSM_LANES)], axis=-1)
        h0_rows = jnp.pad(h0, ((0, 0), (0, dec_seq - 1), (0, 0))).reshape(n_s, 2 * SSM_LANES)
        y, h_all = _ssm_scan(u, h0_rows, wb, wc, pw, d_row, l, dec_seq)
        xs = _mixer_out(xs, mod_s[5], olat, osb, y, wuv_bd, wglu, bglu, nm, ns, nz, wout, ln_g3, ln_b3, l, n_s)
        xs = _ffn(xs, mod_s[6], mod_s[7], mod_s[8], *ffn_w[1], ln_g3, ln_b3, l, 2, n_s)
        h_fin = h_all.reshape(dec_batch, dec_seq, 2 * SSM_LANES)[:, dec_seq - 1]
        st_s.append((ckv.reshape(dec_batch, dec_seq, KV_LORA_RANK), kr.reshape(dec_batch, dec_seq, MLA_ROPE_DIM),
                     sk.reshape(dec_batch, dec_seq, SB_HEADS, SB_HEAD_DIM),
                     sv.reshape(dec_batch, dec_seq, SB_HEADS, SB_HEAD_DIM),
                     h_fin[:, :SSM_LANES].reshape(dec_batch, SSM_GROUPS, SSM_STATE),
                     h_fin[:, SSM_LANES:].reshape(dec_batch, SSM_GROUPS, SSM_STATE)))

    stack = lambda sts, i: jnp.stack([s[i] for s in sts], axis=1)
    return (xp.reshape(batch, seq, D_MODEL), xs.reshape(dec_batch, dec_seq, D_MODEL),
            stack(st_p, 0), stack(st_p, 1), stack(st_p, 2), stack(st_p, 3), stack(st_p, 4), stack(st_p, 5),
            stack(st_s, 0), stack(st_s, 1), stack(st_s, 2), stack(st_s, 3), stack(st_s, 4), stack(st_s, 5))
```

```python
import functools
import math

import jax
import jax.numpy as jnp
import numpy as np
from jax import lax
from jax.experimental import pallas as pl
from jax.experimental.pallas import tpu as pltpu

F32 = jnp.float32
BF16 = jnp.bfloat16

D_MODEL = 1024
DEPTH = 2
PAGE_SIZE = 128
MLA_HEADS = 4
MLA_NOPE_DIM = 128
MLA_ROPE_DIM = 64
MLA_V_DIM = 128
Q_LORA_RANK = 256
KV_LORA_RANK = 128
ROPE_BASE = 10000.0
SB_HEADS = 4
SB_HEAD_DIM = 64
SB_WIDTH = SB_HEADS * SB_HEAD_DIM
SSM_WIDTH = 256
SSM_GROUP_CH = 16
SSM_GROUPS = SSM_WIDTH // SSM_GROUP_CH
SSM_STATE = 64
SSM_LANES = SSM_GROUPS * SSM_STATE
D_FF = 2816
N_MOD = 9
MACARON_WEIGHT = 0.5
DEEPNORM_ALPHA = (2 * DEPTH) ** 0.25
LN_EPS = 1e-5
RMS_EPS = 1e-6
MLA_SCALE = (MLA_NOPE_DIM + MLA_ROPE_DIM) ** -0.5
SB_SCALE = SB_HEAD_DIM ** -0.5

LANE = 128
SUBLANE = 8
VMEM_LIMIT_BYTES = 56 * 1024 * 1024

SB_DEAD_LOG = -104.0
NEG_BIG = -1e30

FFN_CHUNK = 256
MLA_TQ = 256
MLA_TK = 1024
SB_TQ = 256
SB_TK = 128
SSM_CHUNK = 512
DEC_PAGES = 32


def _cparams(sem):
    return pltpu.CompilerParams(dimension_semantics=sem, vmem_limit_bytes=VMEM_LIMIT_BYTES)


def _dot(a, b):
    return jnp.dot(a, b, preferred_element_type=F32)


def _dot_nt(a, b):
    return lax.dot_general(a, b, (((1,), (1,)), ((), ())), preferred_element_type=F32)


def _rms(x, g):
    return x * lax.rsqrt(jnp.mean(x * x, axis=-1, keepdims=True) + RMS_EPS) * g


def _layer_norm(y, g, b):
    mu = jnp.mean(y, axis=-1, keepdims=True)
    yc = y - mu
    var = jnp.mean(yc * yc, axis=-1, keepdims=True)
    return yc * lax.rsqrt(var + LN_EPS) * g + b


def _ada_kernel(c_ref, w_ref, b_ref, o_ref):
    c = c_ref[...]
    s = (c * jax.nn.sigmoid(c)).astype(BF16)
    o_ref[...] = _dot(s, w_ref[...].astype(BF16)) + b_ref[...]


def _ada(c_all, ada_w, ada_b):
    n = c_all.shape[0]
    depth, _, n_out = ada_w.shape
    tn = D_MODEL
    return pl.pallas_call(
        _ada_kernel,
        grid=(depth, n_out // tn),
        in_specs=[
            pl.BlockSpec((n, D_MODEL), lambda l, j: (0, 0)),
            pl.BlockSpec((None, D_MODEL, tn), lambda l, j: (l, 0, j)),
            pl.BlockSpec((None, 1, tn), lambda l, j: (l, 0, j)),
        ],
        out_specs=pl.BlockSpec((None, n, tn), lambda l, j: (l, 0, j)),
        out_shape=jax.ShapeDtypeStruct((depth, n, n_out), F32),
        compiler_params=_cparams(("parallel", "parallel")),
        name="ada_mod",
    )(c_all, ada_w, ada_b.reshape(depth, 1, n_out))


def _mod_spec(arr, tm, tiles_per_batch):
    if arr.ndim == 3:
        return pl.BlockSpec((None, 1, D_MODEL), lambda i, *_: (i // tiles_per_batch, 0, 0))
    return pl.BlockSpec((tm, D_MODEL), lambda i, *_: (i, 0))


def _ffn_kernel(x_ref, sh_ref, sc_ref, g_ref, wg_ref, wu_ref, wd_ref, lng_ref, lnb_ref, o_ref,
                h_scr, acc_scr):
    j = pl.program_id(1)

    @pl.when(j == 0)
    def _():
        h_scr[...] = (x_ref[...] * (1.0 + sc_ref[...]) + sh_ref[...]).astype(BF16)
        acc_scr[...] = jnp.zeros_like(acc_scr)

    h = h_scr[...]
    gate = _dot(h, wg_ref[...])
    up = _dot(h, wu_ref[...])
    act = (gate * jax.nn.sigmoid(gate) * up).astype(BF16)
    acc_scr[...] += _dot(act, wd_ref[...])

    @pl.when(j == pl.num_programs(1) - 1)
    def _():
        y = DEEPNORM_ALPHA * x_ref[...] + MACARON_WEIGHT * (1.0 + g_ref[...]) * acc_scr[...]
        o_ref[...] = _layer_norm(y, lng_ref[...], lnb_ref[...])


def _ffn(x, sh, sc, g, wg, wu, wd, ln_g3, ln_b3, layer, which, tokens_per_batch):
    n = x.shape[0]
    tm = min(1024, tokens_per_batch)
    tpb = max(tokens_per_batch // tm, 1)
    n_chunks = D_FF // FFN_CHUNK
    ln_idx = layer * 3 + which
    return pl.pallas_call(
        _ffn_kernel,
        grid=(n // tm, n_chunks),
        in_specs=[
            pl.BlockSpec((tm, D_MODEL), lambda i, j: (i, 0)),
            _mod_spec(sh, tm, tpb), _mod_spec(sc, tm, tpb), _mod_spec(g, tm, tpb),
            pl.BlockSpec((None, D_MODEL, FFN_CHUNK), lambda i, j: (layer, 0, j)),
            pl.BlockSpec((None, D_MODEL, FFN_CHUNK), lambda i, j: (layer, 0, j)),
            pl.BlockSpec((None, FFN_CHUNK, D_MODEL), lambda i, j: (layer, j, 0)),
            pl.BlockSpec((None, 1, D_MODEL), lambda i, j: (ln_idx, 0, 0)),
            pl.BlockSpec((None, 1, D_MODEL), lambda i, j: (ln_idx, 0, 0)),
        ],
        out_specs=pl.BlockSpec((tm, D_MODEL), lambda i, j: (i, 0)),
        out_shape=jax.ShapeDtypeStruct((n, D_MODEL), F32),
        scratch_shapes=[pltpu.VMEM((tm, D_MODEL), BF16), pltpu.VMEM((tm, D_MODEL), F32)],
        compiler_params=_cparams(("parallel", "arbitrary")),
        name="ffn_ln",
    )(x, sh, sc, g, wg, wu, wd, ln_g3, ln_b3)


W_IN_EXT = 1536


def _mixer_in_kernel(x_ref, sh_ref, sc_ref, tab_ref, win_ref, qn_ref, kvn_ref, wq_ref, wuk_ref,
                     q_ref, kpad_ref, ckv_ref, kr_ref, sq_ref, sk_ref, sv_ref, skb_ref, svb_ref, u_ref):
    h = (x_ref[...] * (1.0 + sc_ref[...]) + sh_ref[...]).astype(BF16)
    p = _dot(h, win_ref[...])
    tab = tab_ref[...]
    lane = lax.broadcasted_iota(jnp.int32, tab.shape, 1)

    def rope_pair(pair):
        t = pair * tab
        return t + pltpu.roll(t, MLA_ROPE_DIM, 1)

    ckv = _rms(p[:, 256:384], kvn_ref[...])
    kr = rope_pair(p[:, 384:512])
    ckv_ref[...] = ckv
    kr_ref[...] = kr[:, :MLA_ROPE_DIM]
    kpad_ref[:, 0:128] = ckv.astype(BF16)
    kpad_ref[:, 128:256] = jnp.where(lane < MLA_ROPE_DIM, kr, 0.0).astype(BF16)

    sq_ref[...] = (p[:, 512:768] * SB_SCALE).astype(BF16)
    sk = p[:, 768:1024]
    sv = p[:, 1024:1280]
    sk_ref[...] = sk
    sv_ref[...] = sv
    skb_ref[...] = sk.astype(BF16)
    svb_ref[...] = sv.astype(BF16)
    u_ref[...] = p[:, 1280:1536]

    cq = _rms(p[:, 0:256], qn_ref[...]).astype(BF16)
    qa = _dot(cq, wq_ref[...])
    for pr in range(MLA_HEADS // 2):
        lat2 = _dot(qa[:, 256 * pr:256 * (pr + 1)].astype(BF16), wuk_ref[pr])
        for hh in range(2):
            hd = 2 * pr + hh
            q_ref[:, 256 * hd:256 * hd + 128] = lat2[:, 128 * hh:128 * (hh + 1)].astype(BF16)
            q_ref[:, 256 * hd + 128:256 * hd + 256] = rope_pair(
                qa[:, 512 + 128 * hd:640 + 128 * hd]).astype(BF16)


def _mixer_in(x, sh, sc, tab, tab_rows_per_batch, win, qn, kvn, wq, wuk, layer, tokens_per_batch):
    n = x.shape[0]
    tm = min(512, tokens_per_batch)
    tpb = max(tokens_per_batch // tm, 1)
    tab_tiles = tab.shape[0] // tm

    def wspec(shape):
        nd = len(shape)
        return pl.BlockSpec((None,) + shape, lambda i: (layer,) + (0,) * nd)

    def ospec(w):
        return pl.BlockSpec((tm, w), lambda i: (i, 0))

    outs = [(4 * 256, BF16), (256, BF16), (KV_LORA_RANK, F32), (MLA_ROPE_DIM, F32), (256, BF16),
            (256, F32), (256, F32), (256, BF16), (256, BF16), (256, F32)]
    return pl.pallas_call(
        _mixer_in_kernel,
        grid=(n // tm,),
        in_specs=[
            pl.BlockSpec((tm, D_MODEL), lambda i: (i, 0)),
            _mod_spec(sh, tm, tpb), _mod_spec(sc, tm, tpb),
            pl.BlockSpec((tm, LANE), lambda i: (i % tab_tiles, 0)),
            wspec((D_MODEL, W_IN_EXT)), wspec((1, Q_LORA_RANK)), wspec((1, KV_LORA_RANK)),
            wspec((Q_LORA_RANK, 1024)), wspec((2, 256, 256)),
        ],
        out_specs=[ospec(w) for w, _ in outs],
        out_shape=[jax.ShapeDtypeStruct((n, w), dt) for w, dt in outs],
        compiler_params=_cparams(("parallel",)),
        name="mixer_in",
    )(x, sh, sc, tab, win, qn, kvn, wq, wuk)


def _mla_prompt_kernel(q_ref, k_ref, o_ref, qs_scr, m_scr, l_scr, acc_scr):
    qi = pl.program_id(1)
    rows = MLA_HEADS * MLA_TQ
    for hd in range(MLA_HEADS):
        qs_scr[hd * MLA_TQ:(hd + 1) * MLA_TQ, :] = q_ref[:, 256 * hd:256 * (hd + 1)]
    m_scr[...] = jnp.full_like(m_scr, NEG_BIG)
    l_scr[...] = jnp.zeros_like(l_scr)
    acc_scr[...] = jnp.zeros_like(acc_scr)

    def block(j, masked):
        kblk = k_ref[pl.ds(pl.multiple_of(j * MLA_TK, MLA_TK), MLA_TK), :]
        s = _dot_nt(qs_scr[...], kblk) * MLA_SCALE
        if masked:
            row = (lax.broadcasted_iota(jnp.int32, (rows, MLA_TK), 0) & (MLA_TQ - 1)) + qi * MLA_TQ
            col = lax.broadcasted_iota(jnp.int32, (rows, MLA_TK), 1) + j * MLA_TK
            s = jnp.where(col <= row, s, NEG_BIG)
        m_prev = m_scr[...]
        m_new = jnp.maximum(m_prev, jnp.max(s, axis=1, keepdims=True))
        alpha = jnp.exp(m_prev - m_new)
        pexp = jnp.exp(s - m_new)
        l_scr[...] = alpha * l_scr[...] + jnp.sum(pexp, axis=1, keepdims=True)
        acc_scr[...] = alpha * acc_scr[...] + _dot(pexp.astype(BF16), kblk[:, :KV_LORA_RANK])
        m_scr[...] = m_new

    def body(j, carry):
        block(j, False)
        return carry

    n_full = (qi * MLA_TQ) // MLA_TK
    lax.fori_loop(0, n_full, body, 0)
    for d in range(max(MLA_TQ // MLA_TK, 1)):
        block(n_full + d, True)
    out = acc_scr[...] / l_scr[...]
    for hd in range(MLA_HEADS):
        o_ref[:, 128 * hd:128 * (hd + 1)] = out[hd * MLA_TQ:(hd + 1) * MLA_TQ, :]


def _mla_prompt(q, kpad, batch, seq):
    n = q.shape[0]
    nq = seq // MLA_TQ
    return pl.pallas_call(
        _mla_prompt_kernel,
        grid=(batch, nq),
        in_specs=[
            pl.BlockSpec((MLA_TQ, 1024), lambda b, i: (b * nq + i, 0)),
            pl.BlockSpec((seq, 256), lambda b, i: (b, 0)),
        ],
        out_specs=pl.BlockSpec((MLA_TQ, MLA_HEADS * KV_LORA_RANK), lambda b, i: (b * nq + i, 0)),
        out_shape=jax.ShapeDtypeStruct((n, MLA_HEADS * KV_LORA_RANK), F32),
        scratch_shapes=[pltpu.VMEM((MLA_HEADS * MLA_TQ, 256), BF16),
                        pltpu.VMEM((MLA_HEADS * MLA_TQ, 1), F32), pltpu.VMEM((MLA_HEADS * MLA_TQ, 1), F32),
                        pltpu.VMEM((MLA_HEADS * MLA_TQ, KV_LORA_RANK), F32)],
        compiler_params=_cparams(("parallel", "parallel")),
        name="mla_prompt",
    )(q, kpad)


def _lane_head(shape):
    return lax.shift_right_logical(lax.broadcasted_iota(jnp.int32, shape, 1), int(math.log2(SB_HEAD_DIM)))


def _suffix_ones(tk):
    r = lax.broadcasted_iota(jnp.int32, (tk, tk), 0)
    c = lax.broadcasted_iota(jnp.int32, (tk, tk), 1)
    return jnp.where(r > c, 1.0, 0.0).astype(BF16)


def _sb_logs(z):
    sp = jnp.log1p(jnp.exp(-jnp.abs(z)))
    log_beta = jnp.minimum(z, 0.0) - sp
    log_keep = -jnp.maximum(z, 0.0) - sp
    return log_beta, log_keep


def _sb_block(z, carry, umat, allowed):
    log_beta, lk = _sb_logs(z)
    if allowed is not None:
        lk = jnp.where(allowed, lk, 0.0)
    hi = lk.astype(BF16)
    lo = (lk - hi.astype(F32)).astype(BF16)
    excl = _dot(hi, umat) + _dot(lo, umat)
    w = jnp.exp(log_beta + carry + excl)
    if allowed is not None:
        w = jnp.where(allowed, w, 0.0)
    return w, carry + excl[:, 0:1] + lk[:, 0:1]


def _sb_prompt_kernel(q_ref, k_ref, v_ref, o_ref, qm_scr, c_scr, acc_scr):
    qi = pl.program_id(1)
    lane_head = _lane_head((SB_TQ, SB_WIDTH))
    q = q_ref[...]
    for hd in range(SB_HEADS):
        qm_scr[hd] = jnp.where(lane_head == hd, q, jnp.zeros_like(q))
    c_scr[...] = jnp.zeros_like(c_scr)
    acc_scr[...] = jnp.zeros_like(acc_scr)
    umat = _suffix_ones(SB_TK)

    def block(j, masked):
        start = pl.multiple_of(j * SB_TK, SB_TK)
        kblk = k_ref[pl.ds(start, SB_TK), :]
        vblk = v_ref[pl.ds(start, SB_TK), :]
        allowed = None
        if masked:
            row = lax.broadcasted_iota(jnp.int32, (SB_TQ, SB_TK), 0) + qi * SB_TQ
            col = lax.broadcasted_iota(jnp.int32, (SB_TQ, SB_TK), 1) + j * SB_TK
            allowed = col < row
        acc = acc_scr[...]
        for hd in range(SB_HEADS):
            z = _dot_nt(qm_scr[hd], kblk)
            w, c_new = _sb_block(z, c_scr[hd], umat, allowed)
            c_scr[hd] = c_new
            acc = acc + jnp.where(lane_head == hd, _dot(w.astype(BF16), vblk), 0.0)
        acc_scr[...] = acc

    n_diag = SB_TQ // SB_TK
    j_top = (qi + 1) * n_diag - 1
    for d in range(n_diag):
        block(j_top - d, True)

    def alive():
        c_max = functools.reduce(jnp.maximum, [c_scr[hd] for hd in range(SB_HEADS)])
        return (jnp.max(c_max) >= SB_DEAD_LOG).astype(jnp.int32)

    def cond(carry):
        j, live = carry
        return jnp.logical_and(j >= 0, live > 0)

    def body(carry):
        j, _ = carry
        block(j, False)
        return j - 1, alive()

    lax.while_loop(cond, body, (qi * n_diag - 1, alive()))
    o_ref[...] = acc_scr[...]


def _sb_prompt(sq, skb, svb, batch, seq):
    n = sq.shape[0]
    nq = seq // SB_TQ
    return pl.pallas_call(
        _sb_prompt_kernel,
        grid=(batch, nq),
        in_specs=[
            pl.BlockSpec((SB_TQ, SB_WIDTH), lambda b, i: (b * nq + i, 0)),
            pl.BlockSpec((seq, SB_WIDTH), lambda b, i: (b, 0)),
            pl.BlockSpec((seq, SB_WIDTH), lambda b, i: (b, 0)),
        ],
        out_specs=pl.BlockSpec((SB_TQ, SB_WIDTH), lambda b, i: (b * nq + i, 0)),
        out_shape=jax.ShapeDtypeStruct((n, SB_WIDTH), F32),
        scratch_shapes=[pltpu.VMEM((SB_HEADS, SB_TQ, SB_WIDTH), BF16), pltpu.VMEM((SB_HEADS, SB_TQ, 1), F32),
                        pltpu.VMEM((SB_TQ, SB_WIDTH), F32)],
        compiler_params=_cparams(("parallel", "parallel")),
        name="sb_prompt",
    )(sq, skb, svb)


def _ssm_prep_kernel(lr_ref, li_ref, ldt_ref, lrc_ref, lic_ref, ldtc_ref, bre_ref, bim_ref,
                     pw_ref, bbre_ref, bbim_ref):
    def discretise(lr, li, ldt):
        dt = jnp.exp(ldt)
        mag = jnp.exp(lr * dt)
        a_re = mag * jnp.cos(li * dt)
        a_im = mag * jnp.sin(li * dt)
        den = lr * lr + li * li
        z_re = ((a_re - 1.0) * lr + a_im * li) / den
        z_im = (a_im * lr - (a_re - 1.0) * li) / den
        return a_re, a_im, z_re, z_im

    a_re, a_im, _, _ = discretise(lr_ref[...], li_ref[...], ldt_ref[...])
    p_re, p_im = a_re, a_im
    for k in range(pw_ref.shape[0]):
        pw_ref[k:k + 1, 0:SSM_LANES] = p_re
        pw_ref[k:k + 1, SSM_LANES:2 * SSM_LANES] = p_im
        p_re, p_im = p_re * a_re - p_im * a_im, p_re * a_im + p_im * a_re
    _, _, z_re, z_im = discretise(lrc_ref[...], lic_ref[...], ldtc_ref[...])
    bre, bim = bre_ref[...], bim_ref[...]
    bbre_ref[...] = z_re * bre - z_im * bim
    bbim_ref[...] = z_re * bim + z_im * bre


def _ssm_prep(lam_re, lam_im, log_dt, b_re, b_im, n_pw):
    depth = lam_re.shape[0]
    ldt = jnp.repeat(log_dt, SSM_STATE, axis=1)
    rows = [a.reshape(depth, 1, SSM_LANES) for a in (lam_re, lam_im, ldt)]
    cols = [a.reshape(depth, SSM_LANES, 1) for a in (lam_re, lam_im, ldt)]
    bs = [a.reshape(depth, SSM_LANES, SSM_GROUP_CH) for a in (b_re, b_im)]
    row_spec = pl.BlockSpec((None, 1, SSM_LANES), lambda l: (l, 0, 0))
    col_spec = pl.BlockSpec((None, SSM_LANES, 1), lambda l: (l, 0, 0))
    b_spec = pl.BlockSpec((None, SSM_LANES, SSM_GROUP_CH), lambda l: (l, 0, 0))
    return pl.pallas_call(
        _ssm_prep_kernel,
        grid=(depth,),
        in_specs=[row_spec] * 3 + [col_spec] * 3 + [b_spec] * 2,
        out_specs=[pl.BlockSpec((None, n_pw, 2 * SSM_LANES), lambda l: (l, 0, 0)), b_spec, b_spec],
        out_shape=[jax.ShapeDtypeStruct((depth, n_pw, 2 * SSM_LANES), F32),
                   jax.ShapeDtypeStruct((depth, SSM_LANES, SSM_GROUP_CH), F32),
                   jax.ShapeDtypeStruct((depth, SSM_LANES, SSM_GROUP_CH), F32)],
        compiler_params=_cparams(("parallel",)),
        name="ssm_prep",
    )(*rows, *cols, *bs)


def _ssm_scan_kernel(seg_len, chunks_per_seg, has_h0, *refs):
    if has_h0:
        u_ref, h0_ref, wb_ref, wc_ref, pw_ref, d_ref, y_ref, h_ref, buf, carry_scr = refs
    else:
        u_ref, wb_ref, wc_ref, pw_ref, d_ref, y_ref, h_ref, buf, carry_scr = refs
    i = pl.program_id(0)
    rows = u_ref.shape[0]
    n_tiles = rows // SUBLANE
    width = 2 * SSM_LANES
    tile_shape = (SUBLANE, SSM_LANES)

    @pl.when(i == 0)
    def _():
        carry_scr[...] = jnp.zeros_like(carry_scr)

    def cmul_add(base_re, base_im, a_re, a_im, v_re, v_im):
        return (base_re + a_re * v_re - a_im * v_im, base_im + a_re * v_im + a_im * v_re)

    def pw(r0, r1):
        return pw_ref[r0:r1, 0:SSM_LANES], pw_ref[r0:r1, SSM_LANES:width]

    u = u_ref[...]
    x = _dot(u.astype(BF16), wb_ref[...])
    chain = seg_len > SUBLANE
    if has_h0:
        h0 = h0_ref[...]
        x_re, x_im = cmul_add(x[:, :SSM_LANES], x[:, SSM_LANES:], *pw(0, 1),
                              h0[:, :SSM_LANES], h0[:, SSM_LANES:])
        buf[:, 0:SSM_LANES] = x_re
        buf[:, SSM_LANES:width] = x_im
        carry0 = (jnp.zeros(tile_shape, F32), jnp.zeros(tile_shape, F32))
    else:
        buf[...] = x
        first = (i % chunks_per_seg) == 0
        carry0 = (jnp.broadcast_to(jnp.where(first, 0.0, carry_scr[:, 0:SSM_LANES]), tile_shape),
                  jnp.broadcast_to(jnp.where(first, 0.0, carry_scr[:, SSM_LANES:width]), tile_shape))

    span = min(seg_len, SUBLANE)
    row_in_seg = lax.broadcasted_iota(jnp.int32, tile_shape, 0) & (span - 1)

    def tile_step(r, carry):
        base = pl.multiple_of(r * SUBLANE, SUBLANE)
        t_re = buf[pl.ds(base, SUBLANE), 0:SSM_LANES]
        t_im = buf[pl.ds(base, SUBLANE), SSM_LANES:width]
        d = 1
        while d < span:
            keep = row_in_seg >= d
            sh_re = jnp.where(keep, pltpu.roll(t_re, d, 0), 0.0)
            sh_im = jnp.where(keep, pltpu.roll(t_im, d, 0), 0.0)
            t_re, t_im = cmul_add(t_re, t_im, *pw(d - 1, d), sh_re, sh_im)
            d *= 2
        if chain:
            t_re, t_im = cmul_add(t_re, t_im, *pw(0, SUBLANE), *carry)
            carry = (jnp.broadcast_to(t_re[SUBLANE - 1:SUBLANE, :], tile_shape),
                     jnp.broadcast_to(t_im[SUBLANE - 1:SUBLANE, :], tile_shape))
        buf[pl.ds(base, SUBLANE), 0:SSM_LANES] = t_re
        buf[pl.ds(base, SUBLANE), SSM_LANES:width] = t_im
        return carry

    c_re, c_im = lax.fori_loop(0, n_tiles, tile_step, carry0, unroll=2)

    y_ref[...] = _dot(buf[...].astype(BF16), wc_ref[...]) + d_ref[...] * u
    if has_h0:
        h_ref[...] = buf[...]
    else:
        carry_scr[:, 0:SSM_LANES] = c_re[0:1, :]
        carry_scr[:, SSM_LANES:width] = c_im[0:1, :]
        h_ref[:, 0:SSM_LANES] = c_re[0:1, :]
        h_ref[:, SSM_LANES:width] = c_im[0:1, :]


def _ssm_scan(u, h0_rows, wb, wc, pw, d_row, layer, seg_len):
    n = u.shape[0]
    rows = min(SSM_CHUNK, n)
    n_chunks = n // rows
    has_h0 = h0_rows is not None
    chunks_per_seg = max(seg_len // rows, 1)
    width = 2 * SSM_LANES
    n_pw = pw.shape[1]
    in_specs = [pl.BlockSpec((rows, SSM_WIDTH), lambda i: (i, 0))]
    args = [u]
    if has_h0:
        in_specs.append(pl.BlockSpec((rows, width), lambda i: (i, 0)))
        args.append(h0_rows)
        h_spec = pl.BlockSpec((rows, width), lambda i: (i, 0))
        h_shape = jax.ShapeDtypeStruct((n, width), F32)
    else:
        h_spec = pl.BlockSpec((None, 1, width), lambda i: (i, 0, 0))
        h_shape = jax.ShapeDtypeStruct((n_chunks, 1, width), F32)
    in_specs += [
        pl.BlockSpec((None, SSM_WIDTH, width), lambda i: (layer, 0, 0)),
        pl.BlockSpec((None, width, SSM_WIDTH), lambda i: (layer, 0, 0)),
        pl.BlockSpec((None, n_pw, width), lambda i: (layer, 0, 0)),
        pl.BlockSpec((None, 1, SSM_WIDTH), lambda i: (layer, 0, 0)),
    ]
    args += [wb, wc, pw, d_row]
    return pl.pallas_call(
        functools.partial(_ssm_scan_kernel, seg_len, chunks_per_seg, has_h0),
        grid=(n_chunks,),
        in_specs=in_specs,
        out_specs=[pl.BlockSpec((rows, SSM_WIDTH), lambda i: (i, 0)), h_spec],
        out_shape=[jax.ShapeDtypeStruct((n, SSM_WIDTH), F32), h_shape],
        scratch_shapes=[pltpu.VMEM((rows, width), F32), pltpu.VMEM((1, width), F32)],
        compiler_params=_cparams(("arbitrary",)),
        name="ssm_scan",
    )(*args)


def _mixer_out_kernel(x_ref, g_ref, olat_ref, osb_ref, y_ref, wuv_ref, wglu_ref, bglu_ref,
                      nm_ref, ns_ref, nz_ref, wout_ref, lng_ref, lnb_ref, o_ref):
    olat = olat_ref[...].astype(BF16)
    o_mla = jnp.concatenate([_dot(olat[:, 0:256], wuv_ref[0]), _dot(olat[:, 256:512], wuv_ref[1])], axis=1)
    yg = jax.nn.gelu(y_ref[...], approximate=True)
    o_ssm = yg * jax.nn.sigmoid(_dot(yg.astype(BF16), wglu_ref[...]) + bglu_ref[...])
    merged = jnp.concatenate([_rms(o_mla, nm_ref[...]), _rms(osb_ref[...], ns_ref[...]),
                              _rms(o_ssm, nz_ref[...])], axis=1).astype(BF16)
    m = _dot(merged, wout_ref[...])
    y = DEEPNORM_ALPHA * x_ref[...] + (1.0 + g_ref[...]) * m
    o_ref[...] = _layer_norm(y, lng_ref[...], lnb_ref[...])


def _mixer_out(x, g, olat, osb, y, wuv, wglu, bglu, nm, ns, nz, wout, ln_g3, ln_b3, layer, tokens_per_batch):
    n = x.shape[0]
    tm = min(512, tokens_per_batch)
    tpb = max(tokens_per_batch // tm, 1)
    ln_idx = layer * 3 + 1

    def wspec(shape):
        nd = len(shape)
        return pl.BlockSpec((None,) + shape, lambda i: (layer,) + (0,) * nd)

    return pl.pallas_call(
        _mixer_out_kernel,
        grid=(n // tm,),
        in_specs=[
            pl.BlockSpec((tm, D_MODEL), lambda i: (i, 0)),
            _mod_spec(g, tm, tpb),
            pl.BlockSpec((tm, 512), lambda i: (i, 0)),
            pl.BlockSpec((tm, SB_WIDTH), lambda i: (i, 0)),
            pl.BlockSpec((tm, SSM_WIDTH), lambda i: (i, 0)),
            wspec((2, 256, 256)), wspec((SSM_WIDTH, SSM_WIDTH)), wspec((1, SSM_WIDTH)),
            wspec((1, 512)), wspec((1, SB_WIDTH)), wspec((1, SSM_WIDTH)),
            wspec((D_MODEL, D_MODEL)),
            pl.BlockSpec((None, 1, D_MODEL), lambda i: (ln_idx, 0, 0)),
            pl.BlockSpec((None, 1, D_MODEL), lambda i: (ln_idx, 0, 0)),
        ],
        out_specs=pl.BlockSpec((tm, D_MODEL), lambda i: (i, 0)),
        out_shape=jax.ShapeDtypeStruct((n, D_MODEL), F32),
        compiler_params=_cparams(("parallel",)),
        name="mixer_out",
    )(x, g, olat, osb, y, wuv, wglu, bglu, nm, ns, nz, wout, ln_g3, ln_b3)


def _mla_decode_kernel(n_pages, n_pg, layer, pt_ref, q_ref, knew_ref, ckv_hbm, krt_hbm, o_ref,
                       ckv_buf, kr_buf, sem):
    b = pl.program_id(0)
    n_chunks = n_pages // n_pg
    rows = q_ref.shape[0]

    def page_copies(bb, c, j, slot):
        pid = pt_ref[bb * n_pages + c * n_pg + j]
        return (pltpu.make_async_copy(ckv_hbm.at[pid, layer], ckv_buf.at[slot, j], sem.at[0, slot]),
                pltpu.make_async_copy(krt_hbm.at[pid, layer], kr_buf.at[slot, j], sem.at[1, slot]))

    def start_chunk(bb, c, slot):
        def body(j, carry):
            for cp in page_copies(bb, c, j, slot):
                cp.start()
            return carry
        lax.fori_loop(0, n_pg, body, 0)

    def wait_chunk(bb, c, slot):
        def body(j, carry):
            for cp in page_copies(bb, c, j, slot):
                cp.wait()
            return carry
        lax.fori_loop(0, n_pg, body, 0)

    @pl.when(b == 0)
    def _():
        start_chunk(0, 0, 0)

    q = q_ref[...]
    qf = q.astype(F32)
    knew = knew_ref[...].astype(F32)
    tok = lax.shift_right_logical(lax.broadcasted_iota(jnp.int32, (rows, 1), 0), 2)
    s_new = [jnp.where(tok >= s, jnp.sum(qf * knew[s:s + 1, :], axis=1, keepdims=True) * MLA_SCALE, NEG_BIG)
             for s in range(knew.shape[0])]
    m = functools.reduce(jnp.maximum, s_new)
    l = jnp.zeros((rows, 1), F32)
    acc = jnp.zeros((rows, KV_LORA_RANK), F32)
    for s in range(knew.shape[0]):
        pexp = jnp.exp(s_new[s] - m)
        l = l + pexp
        acc = acc + pexp.astype(BF16).astype(F32) * knew[s:s + 1, 0:KV_LORA_RANK]

    q_lat = q[:, 0:KV_LORA_RANK]
    q_rope = q[:, KV_LORA_RANK:KV_LORA_RANK + MLA_ROPE_DIM]
    for c in range(n_chunks):
        slot = c % 2
        wait_chunk(b, c, slot)
        if c + 1 < n_chunks:
            start_chunk(b, c + 1, 1 - slot)
        else:
            @pl.when(b + 1 < pl.num_programs(0))
            def _():
                start_chunk(b + 1, 0, 0)

        ckv = [ckv_buf[slot, j].astype(BF16) for j in range(n_pg)]
        s = jnp.concatenate(
            [_dot_nt(q_lat, ckv[j]) + _dot(q_rope, kr_buf[slot, j].astype(BF16)) for j in range(n_pg)],
            axis=1) * MLA_SCALE
        m_new = jnp.maximum(m, jnp.max(s, axis=1, keepdims=True))
        alpha = jnp.exp(m - m_new)
        pexp = jnp.exp(s - m_new)
        l = alpha * l + jnp.sum(pexp, axis=1, keepdims=True)
        pb = pexp.astype(BF16)
        pv = _dot(pb[:, 0:PAGE_SIZE], ckv[0])
        for j in range(1, n_pg):
            pv = pv + _dot(pb[:, j * PAGE_SIZE:(j + 1) * PAGE_SIZE], ckv[j])
        acc = alpha * acc + pv
        m = m_new
    o_ref[...] = acc / l


def _mla_decode(page_table_flat, q16, knew, cache_ckv, cache_krope_t, layer, n_pages):
    nb, rows, _ = q16.shape
    n_pg = min(DEC_PAGES, n_pages // 2)
    assert n_pages % (2 * n_pg) == 0
    grid_spec = pltpu.PrefetchScalarGridSpec(
        num_scalar_prefetch=1,
        grid=(nb,),
        in_specs=[pl.BlockSpec((None, rows, 256), lambda b, pt: (b, 0, 0)),
                  pl.BlockSpec((None, knew.shape[1], 256), lambda b, pt: (b, 0, 0)),
                  pl.BlockSpec(memory_space=pl.ANY),
                  pl.BlockSpec(memory_space=pl.ANY)],
        out_specs=pl.BlockSpec((None, rows, KV_LORA_RANK), lambda b, pt: (b, 0, 0)),
        scratch_shapes=[pltpu.VMEM((2, n_pg, PAGE_SIZE, KV_LORA_RANK), F32),
                        pltpu.VMEM((2, n_pg, MLA_ROPE_DIM, PAGE_SIZE), F32),
                        pltpu.SemaphoreType.DMA((2, 2))],
    )
    return pl.pallas_call(
        functools.partial(_mla_decode_kernel, n_pages, n_pg, layer),
        grid_spec=grid_spec,
        out_shape=jax.ShapeDtypeStruct((nb, rows, KV_LORA_RANK), F32),
        compiler_params=_cparams(("arbitrary",)),
        name="mla_decode",
    )(page_table_flat, q16, knew, cache_ckv, cache_krope_t)


SB_DEC_ROWS = SB_HEADS * SUBLANE


def _sb_decode_kernel(n_pages, layer, pt_ref, q_ref, knew_ref, vnew_ref, k_hbm, v_hbm, o_ref,
                      kbuf, vbuf, sem, c_scr, acc_scr):
    b = pl.program_id(0)

    def copies(p, slot, bb=b):
        pid = pt_ref[bb * n_pages + p]
        return (pltpu.make_async_copy(k_hbm.at[pid, layer], kbuf.at[slot], sem.at[0, slot]),
                pltpu.make_async_copy(v_hbm.at[pid, layer], vbuf.at[slot], sem.at[1, slot]))

    def start(p, slot, bb=b):
        for cp in copies(p, slot, bb):
            cp.start()

    def wait(p, slot):
        for cp in copies(p, slot):
            cp.wait()

    @pl.when(b == 0)
    def _():
        start(n_pages - 1, 0)

    q = q_ref[...]
    qf = q.astype(F32)
    tok = lax.broadcasted_iota(jnp.int32, (SB_DEC_ROWS, 1), 0) & (SUBLANE - 1)
    n_new = knew_ref.shape[0]
    knew = knew_ref[...].astype(F32)
    vnew = vnew_ref[...].astype(F32)
    carry = jnp.where(tok < n_new, 0.0, NEG_BIG)
    acc = jnp.zeros((SB_DEC_ROWS, SB_WIDTH), F32)
    for s in range(n_new - 1, -1, -1):
        z = jnp.sum(qf * knew[s:s + 1, :], axis=1, keepdims=True)
        log_beta, lk = _sb_logs(z)
        allowed = tok > s
        w = jnp.where(allowed, jnp.exp(log_beta + carry), 0.0)
        carry = carry + jnp.where(allowed, lk, 0.0)
        acc = acc + w.astype(BF16).astype(F32) * vnew[s:s + 1, :]
    c_scr[...] = carry
    acc_scr[...] = acc
    umat = _suffix_ones(PAGE_SIZE)

    def alive():
        return (jnp.max(c_scr[...]) >= SB_DEAD_LOG).astype(jnp.int32)

    def cond(state):
        p, live = state
        return jnp.logical_and(p >= 0, live > 0)

    def body(state):
        p, _ = state
        slot = (n_pages - 1 - p) & 1
        wait(p, slot)

        @pl.when(p > 0)
        def _():
            start(p - 1, 1 - slot)

        kpage_t = kbuf[slot].astype(BF16)
        vpage_t = vbuf[slot].astype(BF16)
        z = _dot(q, kpage_t)
        w, c_new = _sb_block(z, c_scr[...], umat, None)
        c_scr[...] = c_new
        acc_scr[...] += _dot_nt(w.astype(BF16), vpage_t)
        return p - 1, alive()

    p_end, _ = lax.while_loop(cond, body, (n_pages - 1, alive()))

    @pl.when(p_end >= 0)
    def _():
        wait(p_end, (n_pages - 1 - p_end) & 1)

    @pl.when(b + 1 < pl.num_programs(0))
    def _():
        start(n_pages - 1, 0, b + 1)

    lane_head = _lane_head((SUBLANE, SB_WIDTH))
    out = jnp.zeros((SUBLANE, SB_WIDTH), F32)
    for hd in range(SB_HEADS):
        out = out + jnp.where(lane_head == hd, acc_scr[hd * SUBLANE:(hd + 1) * SUBLANE, :], 0.0)
    o_ref[...] = out


def _sb_decode(page_table_flat, q32, knew, vnew, cache_k, cache_v, layer, n_pages):
    nb = q32.shape[0]
    n_new = knew.shape[1]
    grid_spec = pltpu.PrefetchScalarGridSpec(
        num_scalar_prefetch=1,
        grid=(nb,),
        in_specs=[pl.BlockSpec((None, SB_DEC_ROWS, SB_WIDTH), lambda b, pt: (b, 0, 0)),
                  pl.BlockSpec((None, n_new, SB_WIDTH), lambda b, pt: (b, 0, 0)),
                  pl.BlockSpec((None, n_new, SB_WIDTH), lambda b, pt: (b, 0, 0)),
                  pl.BlockSpec(memory_space=pl.ANY),
                  pl.BlockSpec(memory_space=pl.ANY)],
        out_specs=pl.BlockSpec((None, SUBLANE, SB_WIDTH), lambda b, pt: (b, 0, 0)),
        scratch_shapes=[pltpu.VMEM((2, SB_WIDTH, PAGE_SIZE), F32), pltpu.VMEM((2, SB_WIDTH, PAGE_SIZE), F32),
                        pltpu.SemaphoreType.DMA((2, 2)),
                        pltpu.VMEM((SB_DEC_ROWS, 1), F32), pltpu.VMEM((SB_DEC_ROWS, SB_WIDTH), F32)],
    )
    return pl.pallas_call(
        functools.partial(_sb_decode_kernel, n_pages, layer),
        grid_spec=grid_spec,
        out_shape=jax.ShapeDtypeStruct((nb, SUBLANE, SB_WIDTH), F32),
        compiler_params=_cparams(("arbitrary",)),
        name="sb_decode",
    )(page_table_flat, q32, knew, vnew, cache_k, cache_v)


def _swap_halves(w):
    half = w.shape[-1] // 2
    return jnp.concatenate([w[..., half:], w[..., :half]], axis=-1)


def _block_diag_pairs(w):
    depth, c, heads, d = w.shape
    wt = jnp.transpose(w, (0, 2, 3, 1)).reshape(depth, heads // 2, 2, d, c)
    eye = jnp.eye(2, dtype=w.dtype)
    return jnp.einsum('lpadc,ab->lpadbc', wt, eye).reshape(depth, heads // 2, 2 * d, 2 * c)


def _rope_table(pos):
    half = MLA_ROPE_DIM // 2
    freqs = ROPE_BASE ** (-jnp.arange(half, dtype=F32) / half)
    ang = pos.astype(F32)[:, None] * freqs[None, :]
    cos, sin = jnp.cos(ang), jnp.sin(ang)
    return jnp.concatenate([cos, cos, -sin, sin], axis=-1)


def kernel(x_prompt, x_sample, c_prompt, c_sample, page_table, cache_ckv, cache_krope, cache_sb_k, cache_sb_v, state_ssm_re, state_ssm_im, ada_w, ada_b, ffn1_w_gate, ffn1_w_up, ffn1_w_down, ffn2_w_gate, ffn2_w_up, ffn2_w_down, ln_g, ln_b, w_in, mla_q_norm, mla_kv_norm, mla_w_uq, mla_w_uk, mla_w_uv, ssm_lam_re, ssm_lam_im, ssm_log_dt, ssm_b_re, ssm_b_im, ssm_c_re, ssm_c_im, ssm_d, ssm_w_glu, ssm_b_glu, out_norm_mla, out_norm_sb, out_norm_ssm, w_out):
    batch, seq, _ = x_prompt.shape
    dec_batch, dec_seq, _ = x_sample.shape
    n_pages = page_table.shape[1]
    past_len = n_pages * PAGE_SIZE
    n_p, n_s = batch * seq, dec_batch * dec_seq
    depth = ada_w.shape[0]
    assert seq % SSM_CHUNK == 0 and seq % MLA_TQ == 0 and seq % MLA_TK == 0 and seq % SB_TQ == 0
    assert n_pages % 2 == 0 and dec_seq < SUBLANE and (dec_seq & (dec_seq - 1)) == 0
    assert n_s <= SSM_CHUNK and n_s % SUBLANE == 0

    bf = lambda a: a.astype(BF16)
    ffn_w = [(bf(ffn1_w_gate), bf(ffn1_w_up), bf(ffn1_w_down)), (bf(ffn2_w_gate), bf(ffn2_w_up), bf(ffn2_w_down))]
    kr_cols = w_in[:, :, 384:448]
    win_ext = bf(jnp.concatenate([w_in[:, :, 0:448], _swap_halves(kr_cols), w_in[:, :, 448:]], axis=-1))
    nope = mla_w_uq[..., :MLA_NOPE_DIM].reshape(depth, Q_LORA_RANK, MLA_HEADS * MLA_NOPE_DIM)
    rope_w = mla_w_uq[..., MLA_NOPE_DIM:]
    rope_pairs = jnp.concatenate([rope_w, _swap_halves(rope_w)], axis=-1).reshape(depth, Q_LORA_RANK, MLA_HEADS * 128)
    wq = bf(jnp.concatenate([nope, rope_pairs], axis=-1))
    wuk_bd = bf(_block_diag_pairs(mla_w_uk))
    wuv_bd = bf(_block_diag_pairs(jnp.transpose(mla_w_uv, (0, 3, 2, 1))))
    wglu = bf(ssm_w_glu)
    wout = bf(w_out)
    row = lambda a: a.reshape(depth, 1, a.shape[-1])
    qn, kvn = row(mla_q_norm), row(mla_kv_norm)
    nm, ns, nz, bglu = row(out_norm_mla), row(out_norm_sb), row(out_norm_ssm), row(ssm_b_glu)
    d_row = ssm_d.reshape(depth, 1, SSM_WIDTH)
    ln_g3 = ln_g.reshape(depth * 3, 1, D_MODEL)
    ln_b3 = ln_b.reshape(depth * 3, 1, D_MODEL)

    pw, bb_re, bb_im = _ssm_prep(ssm_lam_re, ssm_lam_im, ssm_log_dt, ssm_b_re, ssm_b_im, SUBLANE)
    eye_g = jnp.eye(SSM_GROUPS, dtype=F32)

    def b_map(bb):
        bb = bb.reshape(depth, SSM_GROUPS, SSM_STATE, SSM_GROUP_CH)
        return jnp.einsum('lgph,gk->lghkp', bb, eye_g).reshape(depth, SSM_WIDTH, SSM_LANES)

    def c_map(cc):
        return jnp.einsum('lghp,gk->lgpkh', cc, eye_g).reshape(depth, SSM_LANES, SSM_WIDTH)

    wb = bf(jnp.concatenate([b_map(bb_re), b_map(bb_im)], axis=-1))
    wc = bf(jnp.concatenate([c_map(ssm_c_re), -c_map(ssm_c_im)], axis=1))

    mod = _ada(jnp.concatenate([c_prompt, c_sample], axis=0), ada_w, ada_b)
    tab_p = _rope_table(jnp.arange(seq))
    tab_s = jnp.tile(_rope_table(past_len + jnp.arange(dec_seq)), (dec_batch, 1))
    pt_flat = page_table.reshape(-1).astype(jnp.int32)
    cache_krope_t = jnp.swapaxes(cache_krope, 2, 3)
    sb_page_t = lambda c: jnp.transpose(c, (0, 1, 3, 4, 2)).reshape(c.shape[:2] + (SB_WIDTH, PAGE_SIZE))
    cache_k4, cache_v4 = sb_page_t(cache_sb_k), sb_page_t(cache_sb_v)
    head_mask = (jnp.arange(SB_WIDTH)[None, :] // SB_HEAD_DIM == jnp.arange(SB_HEADS)[:, None])

    xp = x_prompt.reshape(n_p, D_MODEL)
    xs = x_sample.reshape(n_s, D_MODEL)
    st_p, st_s = [], []
    for l in range(depth):
        mp = mod[l, :batch].reshape(batch, N_MOD, 1, D_MODEL)
        ms = jnp.repeat(mod[l, batch:], dec_seq, axis=0).reshape(n_s, N_MOD, D_MODEL)
        mod_p = [mp[:, k] for k in range(N_MOD)]
        mod_s = [ms[:, k] for k in range(N_MOD)]

        xp = _ffn(xp, mod_p[0], mod_p[1], mod_p[2], *ffn_w[0], ln_g3, ln_b3, l, 0, seq)
        q, kpad, ckv, kr, sq, sk, sv, skb, svb, u = _mixer_in(
            xp, mod_p[3], mod_p[4], tab_p, seq, win_ext, qn, kvn, wq, wuk_bd, l, seq)
        olat = _mla_prompt(q, kpad, batch, seq)
        osb = _sb_prompt(sq, skb, svb, batch, seq)
        y, h_last = _ssm_scan(u, None, wb, wc, pw, d_row, l, seq)
        xp = _mixer_out(xp, mod_p[5], olat, osb, y, wuv_bd, wglu, bglu, nm, ns, nz, wout, ln_g3, ln_b3, l, seq)
        xp = _ffn(xp, mod_p[6], mod_p[7], mod_p[8], *ffn_w[1], ln_g3, ln_b3, l, 2, seq)
        cps = seq // SSM_CHUNK
        h_fin = h_last.reshape(batch, cps, 2 * SSM_LANES)[:, cps - 1]
        st_p.append((ckv.reshape(batch, seq, KV_LORA_RANK), kr.reshape(batch, seq, MLA_ROPE_DIM),
                     sk.reshape(batch, seq, SB_HEADS, SB_HEAD_DIM), sv.reshape(batch, seq, SB_HEADS, SB_HEAD_DIM),
                     h_fin[:, :SSM_LANES].reshape(batch, SSM_GROUPS, SSM_STATE),
                     h_fin[:, SSM_LANES:].reshape(batch, SSM_GROUPS, SSM_STATE)))

        xs = _ffn(xs, mod_s[0], mod_s[1], mod_s[2], *ffn_w[0], ln_g3, ln_b3, l, 0, n_s)
        q, kpad, ckv, kr, sq, sk, sv, skb, svb, u = _mixer_in(
            xs, mod_s[3], mod_s[4], tab_s, n_s, win_ext, qn, kvn, wq, wuk_bd, l, n_s)
        q16 = q.reshape(dec_batch, dec_seq * MLA_HEADS, 256)
        olat = _mla_decode(pt_flat, q16, kpad.reshape(dec_batch, dec_seq, 256), cache_ckv, cache_krope_t, l, n_pages)
        olat = olat.reshape(n_s, MLA_HEADS * KV_LORA_RANK)
        sq3 = sq.reshape(dec_batch, 1, dec_seq, SB_WIDTH)
        q32 = jnp.where(head_mask[None, :, None, :], sq3, jnp.zeros_like(sq3))
        q32 = jnp.pad(q32, ((0, 0), (0, 0), (0, SUBLANE - dec_seq), (0, 0))).reshape(dec_batch, SB_DEC_ROWS, SB_WIDTH)
        osb = _sb_decode(pt_flat, q32, skb.reshape(dec_batch, dec_seq, SB_WIDTH),
                         svb.reshape(dec_batch, dec_seq, SB_WIDTH), cache_k4, cache_v4, l, n_pages)
        osb = osb[:, :dec_seq].reshape(n_s, SB_WIDTH)
        h0 = jnp.concatenate([state_ssm_re[:, l].reshape(dec_batch, 1, SSM_LANES),
                              state_ssm_im[:, l].reshape(dec_batch, 1, SSM_LANES)], axis=-1)
        h0_rows = jnp.pad(h0, ((0, 0), (0, dec_seq - 1), (0, 0))).reshape(n_s, 2 * SSM_LANES)
        y, h_all = _ssm_scan(u, h0_rows, wb, wc, pw, d_row, l, dec_seq)
        xs = _mixer_out(xs, mod_s[5], olat, osb, y, wuv_bd, wglu, bglu, nm, ns, nz, wout, ln_g3, ln_b3, l, n_s)
        xs = _ffn(xs, mod_s[6], mod_s[7], mod_s[8], *ffn_w[1], ln_g3, ln_b3, l, 2, n_s)
        h_fin = h_all.reshape(dec_batch, dec_seq, 2 * SSM_LANES)[:, dec_seq - 1]
        st_s.append((ckv.reshape(dec_batch, dec_seq, KV_LORA_RANK), kr.reshape(dec_batch, dec_seq, MLA_ROPE_DIM),
                     sk.reshape(dec_batch, dec_seq, SB_HEADS, SB_HEAD_DIM),
                     sv.reshape(dec_batch, dec_seq, SB_HEADS, SB_HEAD_DIM),
                     h_fin[:, :SSM_LANES].reshape(dec_batch, SSM_GROUPS, SSM_STATE),
                     h_fin[:, SSM_LANES:].reshape(dec_batch, SSM_GROUPS, SSM_STATE)))

    stack = lambda sts, i: jnp.stack([s[i] for s in sts], axis=1)
    return (xp.reshape(batch, seq, D_MODEL), xs.reshape(dec_batch, dec_seq, D_MODEL),
            stack(st_p, 0), stack(st_p, 1), stack(st_p, 2), stack(st_p, 3), stack(st_p, 4), stack(st_p, 5),
            stack(st_s, 0), stack(st_s, 1), stack(st_s, 2), stack(st_s, 3), stack(st_s, 4), stack(st_s, 5))
```

```python
import functools
import math

import jax
import jax.numpy as jnp
import numpy as np
from jax import lax
from jax.experimental import pallas as pl
from jax.experimental.pallas import tpu as pltpu

F32 = jnp.float32
BF16 = jnp.bfloat16

D_MODEL = 1024
DEPTH = 2
PAGE_SIZE = 128
MLA_HEADS = 4
MLA_NOPE_DIM = 128
MLA_ROPE_DIM = 64
MLA_V_DIM = 128
Q_LORA_RANK = 256
KV_LORA_RANK = 128
ROPE_BASE = 10000.0
SB_HEADS = 4
SB_HEAD_DIM = 64
SB_WIDTH = SB_HEADS * SB_HEAD_DIM
SSM_WIDTH = 256
SSM_GROUP_CH = 16
SSM_GROUPS = SSM_WIDTH // SSM_GROUP_CH
SSM_STATE = 64
SSM_LANES = SSM_GROUPS * SSM_STATE
D_FF = 2816
N_MOD = 9
MACARON_WEIGHT = 0.5
DEEPNORM_ALPHA = (2 * DEPTH) ** 0.25
LN_EPS = 1e-5
RMS_EPS = 1e-6
MLA_SCALE = (MLA_NOPE_DIM + MLA_ROPE_DIM) ** -0.5
SB_SCALE = SB_HEAD_DIM ** -0.5

LANE = 128
SUBLANE = 8
VMEM_LIMIT_BYTES = 56 * 1024 * 1024

SB_DEAD_LOG = -104.0
NEG_BIG = -1e30

FFN_CHUNK = 256
MLA_TQ = 256
MLA_TK = 1024
SB_TQ = 256
SB_TK = 128
SSM_CHUNK = 512
DEC_PAGES = 32


def _cparams(sem):
    return pltpu.CompilerParams(dimension_semantics=sem, vmem_limit_bytes=VMEM_LIMIT_BYTES)


def _dot(a, b):
    return jnp.dot(a, b, preferred_element_type=F32)


def _dot_nt(a, b):
    return lax.dot_general(a, b, (((1,), (1,)), ((), ())), preferred_element_type=F32)


def _rms(x, g):
    return x * lax.rsqrt(jnp.mean(x * x, axis=-1, keepdims=True) + RMS_EPS) * g


def _layer_norm(y, g, b):
    mu = jnp.mean(y, axis=-1, keepdims=True)
    yc = y - mu
    var = jnp.mean(yc * yc, axis=-1, keepdims=True)
    return yc * lax.rsqrt(var + LN_EPS) * g + b


def _ada_kernel(c_ref, w_ref, b_ref, o_ref):
    c = c_ref[...]
    s = (c * jax.nn.sigmoid(c)).astype(BF16)
    o_ref[...] = _dot(s, w_ref[...].astype(BF16)) + b_ref[...]


def _ada(c_all, ada_w, ada_b):
    n = c_all.shape[0]
    depth, _, n_out = ada_w.shape
    tn = D_MODEL
    return pl.pallas_call(
        _ada_kernel,
        grid=(depth, n_out // tn),
        in_specs=[
            pl.BlockSpec((n, D_MODEL), lambda l, j: (0, 0)),
            pl.BlockSpec((None, D_MODEL, tn), lambda l, j: (l, 0, j)),
            pl.BlockSpec((None, 1, tn), lambda l, j: (l, 0, j)),
        ],
        out_specs=pl.BlockSpec((None, n, tn), lambda l, j: (l, 0, j)),
        out_shape=jax.ShapeDtypeStruct((depth, n, n_out), F32),
        compiler_params=_cparams(("parallel", "parallel")),
        name="ada_mod",
    )(c_all, ada_w, ada_b.reshape(depth, 1, n_out))


def _mod_spec(arr, tm, tiles_per_batch):
    if arr.ndim == 3:
        return pl.BlockSpec((None, 1, D_MODEL), lambda i, *_: (i // tiles_per_batch, 0, 0))
    return pl.BlockSpec((tm, D_MODEL), lambda i, *_: (i, 0))


def _ffn_kernel(x_ref, sh_ref, sc_ref, g_ref, wg_ref, wu_ref, wd_ref, lng_ref, lnb_ref, o_ref,
                h_scr, acc_scr):
    j = pl.program_id(1)

    @pl.when(j == 0)
    def _():
        h_scr[...] = (x_ref[...] * (1.0 + sc_ref[...]) + sh_ref[...]).astype(BF16)
        acc_scr[...] = jnp.zeros_like(acc_scr)

    h = h_scr[...]
    gate = _dot(h, wg_ref[...])
    up = _dot(h, wu_ref[...])
    act = (gate * jax.nn.sigmoid(gate) * up).astype(BF16)
    acc_scr[...] += _dot(act, wd_ref[...])

    @pl.when(j == pl.num_programs(1) - 1)
    def _():
        y = DEEPNORM_ALPHA * x_ref[...] + MACARON_WEIGHT * (1.0 + g_ref[...]) * acc_scr[...]
        o_ref[...] = _layer_norm(y, lng_ref[...], lnb_ref[...])


def _ffn(x, sh, sc, g, wg, wu, wd, ln_g3, ln_b3, layer, which, tokens_per_batch):
    n = x.shape[0]
    tm = min(1024, tokens_per_batch)
    tpb = max(tokens_per_batch // tm, 1)
    n_chunks = D_FF // FFN_CHUNK
    ln_idx = layer * 3 + which
    return pl.pallas_call(
        _ffn_kernel,
        grid=(n // tm, n_chunks),
        in_specs=[
            pl.BlockSpec((tm, D_MODEL), lambda i, j: (i, 0)),
            _mod_spec(sh, tm, tpb), _mod_spec(sc, tm, tpb), _mod_spec(g, tm, tpb),
            pl.BlockSpec((None, D_MODEL, FFN_CHUNK), lambda i, j: (layer, 0, j)),
            pl.BlockSpec((None, D_MODEL, FFN_CHUNK), lambda i, j: (layer, 0, j)),
            pl.BlockSpec((None, FFN_CHUNK, D_MODEL), lambda i, j: (layer, j, 0)),
            pl.BlockSpec((None, 1, D_MODEL), lambda i, j: (ln_idx, 0, 0)),
            pl.BlockSpec((None, 1, D_MODEL), lambda i, j: (ln_idx, 0, 0)),
        ],
        out_specs=pl.BlockSpec((tm, D_MODEL), lambda i, j: (i, 0)),
        out_shape=jax.ShapeDtypeStruct((n, D_MODEL), F32),
        scratch_shapes=[pltpu.VMEM((tm, D_MODEL), BF16), pltpu.VMEM((tm, D_MODEL), F32)],
        compiler_params=_cparams(("parallel", "arbitrary")),
        name="ffn_ln",
    )(x, sh, sc, g, wg, wu, wd, ln_g3, ln_b3)


W_IN_EXT = 1536


def _mixer_in_kernel(x_ref, sh_ref, sc_ref, tab_ref, win_ref, qn_ref, kvn_ref, wq_ref, wuk_ref,
                     q_ref, kpad_ref, ckv_ref, kr_ref, sq_ref, sk_ref, sv_ref, skb_ref, svb_ref, u_ref):
    h = (x_ref[...] * (1.0 + sc_ref[...]) + sh_ref[...]).astype(BF16)
    p = _dot(h, win_ref[...])
    tab = tab_ref[...]
    lane = lax.broadcasted_iota(jnp.int32, tab.shape, 1)

    def rope_pair(pair):
        t = pair * tab
        return t + pltpu.roll(t, MLA_ROPE_DIM, 1)

    ckv = _rms(p[:, 256:384], kvn_ref[...])
    kr = rope_pair(p[:, 384:512])
    ckv_ref[...] = ckv
    kr_ref[...] = kr[:, :MLA_ROPE_DIM]
    kpad_ref[:, 0:128] = ckv.astype(BF16)
    kpad_ref[:, 128:256] = jnp.where(lane < MLA_ROPE_DIM, kr, 0.0).astype(BF16)

    sq_ref[...] = (p[:, 512:768] * SB_SCALE).astype(BF16)
    sk = p[:, 768:1024]
    sv = p[:, 1024:1280]
    sk_ref[...] = sk
    sv_ref[...] = sv
    skb_ref[...] = sk.astype(BF16)
    svb_ref[...] = sv.astype(BF16)
    u_ref[...] = p[:, 1280:1536]

    cq = _rms(p[:, 0:256], qn_ref[...]).astype(BF16)
    qa = _dot(cq, wq_ref[...])
    for pr in range(MLA_HEADS // 2):
        lat2 = _dot(qa[:, 256 * pr:256 * (pr + 1)].astype(BF16), wuk_ref[pr])
        for hh in range(2):
            hd = 2 * pr + hh
            q_ref[:, 256 * hd:256 * hd + 128] = lat2[:, 128 * hh:128 * (hh + 1)].astype(BF16)
            q_ref[:, 256 * hd + 128:256 * hd + 256] = rope_pair(
                qa[:, 512 + 128 * hd:640 + 128 * hd]).astype(BF16)


def _mixer_in(x, sh, sc, tab, tab_rows_per_batch, win, qn, kvn, wq, wuk, layer, tokens_per_batch):
    n = x.shape[0]
    tm = min(512, tokens_per_batch)
    tpb = max(tokens_per_batch // tm, 1)
    tab_tiles = tab.shape[0] // tm

    def wspec(shape):
        nd = len(shape)
        return pl.BlockSpec((None,) + shape, lambda i: (layer,) + (0,) * nd)

    def ospec(w):
        return pl.BlockSpec((tm, w), lambda i: (i, 0))

    outs = [(4 * 256, BF16), (256, BF16), (KV_LORA_RANK, F32), (MLA_ROPE_DIM, F32), (256, BF16),
            (256, F32), (256, F32), (256, BF16), (256, BF16), (256, F32)]
    return pl.pallas_call(
        _mixer_in_kernel,
        grid=(n // tm,),
        in_specs=[
            pl.BlockSpec((tm, D_MODEL), lambda i: (i, 0)),
            _mod_spec(sh, tm, tpb), _mod_spec(sc, tm, tpb),
            pl.BlockSpec((tm, LANE), lambda i: (i % tab_tiles, 0)),
            wspec((D_MODEL, W_IN_EXT)), wspec((1, Q_LORA_RANK)), wspec((1, KV_LORA_RANK)),
            wspec((Q_LORA_RANK, 1024)), wspec((2, 256, 256)),
        ],
        out_specs=[ospec(w) for w, _ in outs],
        out_shape=[jax.ShapeDtypeStruct((n, w), dt) for w, dt in outs],
        compiler_params=_cparams(("parallel",)),
        name="mixer_in",
    )(x, sh, sc, tab, win, qn, kvn, wq, wuk)


def _mla_prompt_kernel(q_ref, k_ref, o_ref, qs_scr, m_scr, l_scr, acc_scr):
    qi = pl.program_id(1)
    rows = MLA_HEADS * MLA_TQ
    for hd in range(MLA_HEADS):
        qs_scr[hd * MLA_TQ:(hd + 1) * MLA_TQ, :] = q_ref[:, 256 * hd:256 * (hd + 1)]
    m_scr[...] = jnp.full_like(m_scr, NEG_BIG)
    l_scr[...] = jnp.zeros_like(l_scr)
    acc_scr[...] = jnp.zeros_like(acc_scr)

    def block(j, masked):
        kblk = k_ref[pl.ds(pl.multiple_of(j * MLA_TK, MLA_TK), MLA_TK), :]
        s = _dot_nt(qs_scr[...], kblk) * MLA_SCALE
        if masked:
            row = (lax.broadcasted_iota(jnp.int32, (rows, MLA_TK), 0) & (MLA_TQ - 1)) + qi * MLA_TQ
            col = lax.broadcasted_iota(jnp.int32, (rows, MLA_TK), 1) + j * MLA_TK
            s = jnp.where(col <= row, s, NEG_BIG)
        m_prev = m_scr[...]
        m_new = jnp.maximum(m_prev, jnp.max(s, axis=1, keepdims=True))
        alpha = jnp.exp(m_prev - m_new)
        pexp = jnp.exp(s - m_new)
        l_scr[...] = alpha * l_scr[...] + jnp.sum(pexp, axis=1, keepdims=True)
        acc_scr[...] = alpha * acc_scr[...] + _dot(pexp.astype(BF16), kblk[:, :KV_LORA_RANK])
        m_scr[...] = m_new

    def body(j, carry):
        block(j, False)
        return carry

    n_full = (qi * MLA_TQ) // MLA_TK
    lax.fori_loop(0, n_full, body, 0)
    for d in range(max(MLA_TQ // MLA_TK, 1)):
        block(n_full + d, True)
    out = acc_scr[...] / l_scr[...]
    for hd in range(MLA_HEADS):
        o_ref[:, 128 * hd:128 * (hd + 1)] = out[hd * MLA_TQ:(hd + 1) * MLA_TQ, :]


def _mla_prompt(q, kpad, batch, seq):
    n = q.shape[0]
    nq = seq // MLA_TQ
    return pl.pallas_call(
        _mla_prompt_kernel,
        grid=(batch, nq),
        in_specs=[
            pl.BlockSpec((MLA_TQ, 1024), lambda b, i: (b * nq + i, 0)),
            pl.BlockSpec((seq, 256), lambda b, i: (b, 0)),
        ],
        out_specs=pl.BlockSpec((MLA_TQ, MLA_HEADS * KV_LORA_RANK), lambda b, i: (b * nq + i, 0)),
        out_shape=jax.ShapeDtypeStruct((n, MLA_HEADS * KV_LORA_RANK), F32),
        scratch_shapes=[pltpu.VMEM((MLA_HEADS * MLA_TQ, 256), BF16),
                        pltpu.VMEM((MLA_HEADS * MLA_TQ, 1), F32), pltpu.VMEM((MLA_HEADS * MLA_TQ, 1), F32),
                        pltpu.VMEM((MLA_HEADS * MLA_TQ, KV_LORA_RANK), F32)],
        compiler_params=_cparams(("parallel", "parallel")),
        name="mla_prompt",
    )(q, kpad)


def _lane_head(shape):
    return lax.shift_right_logical(lax.broadcasted_iota(jnp.int32, shape, 1), int(math.log2(SB_HEAD_DIM)))


def _suffix_ones(tk):
    r = lax.broadcasted_iota(jnp.int32, (tk, tk), 0)
    c = lax.broadcasted_iota(jnp.int32, (tk, tk), 1)
    return jnp.where(r > c, 1.0, 0.0).astype(BF16)


def _sb_logs(z):
    sp = jnp.log1p(jnp.exp(-jnp.abs(z)))
    log_beta = jnp.minimum(z, 0.0) - sp
    log_keep = -jnp.maximum(z, 0.0) - sp
    return log_beta, log_keep


def _sb_block(z, carry, umat, allowed):
    log_beta, lk = _sb_logs(z)
    if allowed is not None:
        lk = jnp.where(allowed, lk, 0.0)
    hi = lk.astype(BF16)
    lo = (lk - hi.astype(F32)).astype(BF16)
    excl = _dot(hi, umat) + _dot(lo, umat)
    w = jnp.exp(log_beta + carry + excl)
    if allowed is not None:
        w = jnp.where(allowed, w, 0.0)
    return w, carry + excl[:, 0:1] + lk[:, 0:1]


def _sb_prompt_kernel(q_ref, k_ref, v_ref, o_ref, qm_scr, c_scr, acc_scr):
    qi = pl.program_id(1)
    lane_head = _lane_head((SB_TQ, SB_WIDTH))
    q = q_ref[...]
    for hd in range(SB_HEADS):
        qm_scr[hd] = jnp.where(lane_head == hd, q, jnp.zeros_like(q))
    c_scr[...] = jnp.zeros_like(c_scr)
    acc_scr[...] = jnp.zeros_like(acc_scr)
    umat = _suffix_ones(SB_TK)

    def block(j, masked):
        start = pl.multiple_of(j * SB_TK, SB_TK)
        kblk = k_ref[pl.ds(start, SB_TK), :]
        vblk = v_ref[pl.ds(start, SB_TK), :]
        allowed = None
        if masked:
            row = lax.broadcasted_iota(jnp.int32, (SB_TQ, SB_TK), 0) + qi * SB_TQ
            col = lax.broadcasted_iota(jnp.int32, (SB_TQ, SB_TK), 1) + j * SB_TK
            allowed = col < row
        acc = acc_scr[...]
        for hd in range(SB_HEADS):
            z = _dot_nt(qm_scr[hd], kblk)
            w, c_new = _sb_block(z, c_scr[hd], umat, allowed)
            c_scr[hd] = c_new
            acc = acc + jnp.where(lane_head == hd, _dot(w.astype(BF16), vblk), 0.0)
        acc_scr[...] = acc

    n_diag = SB_TQ // SB_TK
    j_top = (qi + 1) * n_diag - 1
    for d in range(n_diag):
        block(j_top - d, True)

    def alive():
        c_max = functools.reduce(jnp.maximum, [c_scr[hd] for hd in range(SB_HEADS)])
        return (jnp.max(c_max) >= SB_DEAD_LOG).astype(jnp.int32)

    def cond(carry):
        j, live = carry
        return jnp.logical_and(j >= 0, live > 0)

    def body(carry):
        j, _ = carry
        block(j, False)
        return j - 1, alive()

    lax.while_loop(cond, body, (qi * n_diag - 1, alive()))
    o_ref[...] = acc_scr[...]


def _sb_prompt(sq, skb, svb, batch, seq):
    n = sq.shape[0]
    nq = seq // SB_TQ
    return pl.pallas_call(
        _sb_prompt_kernel,
        grid=(batch, nq),
        in_specs=[
            pl.BlockSpec((SB_TQ, SB_WIDTH), lambda b, i: (b * nq + i, 0)),
            pl.BlockSpec((seq, SB_WIDTH), lambda b, i: (b, 0)),
            pl.BlockSpec((seq, SB_WIDTH), lambda b, i: (b, 0)),
        ],
        out_specs=pl.BlockSpec((SB_TQ, SB_WIDTH), lambda b, i: (b * nq + i, 0)),
        out_shape=jax.ShapeDtypeStruct((n, SB_WIDTH), F32),
        scratch_shapes=[pltpu.VMEM((SB_HEADS, SB_TQ, SB_WIDTH), BF16), pltpu.VMEM((SB_HEADS, SB_TQ, 1), F32),
                        pltpu.VMEM((SB_TQ, SB_WIDTH), F32)],
        compiler_params=_cparams(("parallel", "parallel")),
        name="sb_prompt",
    )(sq, skb, svb)


def _ssm_prep_kernel(lr_ref, li_ref, ldt_ref, lrc_ref, lic_ref, ldtc_ref, bre_ref, bim_ref,
                     pw_ref, bbre_ref, bbim_ref):
    def discretise(lr, li, ldt):
        dt = jnp.exp(ldt)
        mag = jnp.exp(lr * dt)
        a_re = mag * jnp.cos(li * dt)
        a_im = mag * jnp.sin(li * dt)
        den = lr * lr + li * li
        z_re = ((a_re - 1.0) * lr + a_im * li) / den
        z_im = (a_im * lr - (a_re - 1.0) * li) / den
        return a_re, a_im, z_re, z_im

    a_re, a_im, _, _ = discretise(lr_ref[...], li_ref[...], ldt_ref[...])
    p_re, p_im = a_re, a_im
    for k in range(pw_ref.shape[0]):
        pw_ref[k:k + 1, 0:SSM_LANES] = p_re
        pw_ref[k:k + 1, SSM_LANES:2 * SSM_LANES] = p_im
        p_re, p_im = p_re * a_re - p_im * a_im, p_re * a_im + p_im * a_re
    _, _, z_re, z_im = discretise(lrc_ref[...], lic_ref[...], ldtc_ref[...])
    bre, bim = bre_ref[...], bim_ref[...]
    bbre_ref[...] = z_re * bre - z_im * bim
    bbim_ref[...] = z_re * bim + z_im * bre


def _ssm_prep(lam_re, lam_im, log_dt, b_re, b_im, n_pw):
    depth = lam_re.shape[0]
    ldt = jnp.repeat(log_dt, SSM_STATE, axis=1)
    rows = [a.reshape(depth, 1, SSM_LANES) for a in (lam_re, lam_im, ldt)]
    cols = [a.reshape(depth, SSM_LANES, 1) for a in (lam_re, lam_im, ldt)]
    bs = [a.reshape(depth, SSM_LANES, SSM_GROUP_CH) for a in (b_re, b_im)]
    row_spec = pl.BlockSpec((None, 1, SSM_LANES), lambda l: (l, 0, 0))
    col_spec = pl.BlockSpec((None, SSM_LANES, 1), lambda l: (l, 0, 0))
    b_spec = pl.BlockSpec((None, SSM_LANES, SSM_GROUP_CH), lambda l: (l, 0, 0))
    return pl.pallas_call(
        _ssm_prep_kernel,
        grid=(depth,),
        in_specs=[row_spec] * 3 + [col_spec] * 3 + [b_spec] * 2,
        out_specs=[pl.BlockSpec((None, n_pw, 2 * SSM_LANES), lambda l: (l, 0, 0)), b_spec, b_spec],
        out_shape=[jax.ShapeDtypeStruct((depth, n_pw, 2 * SSM_LANES), F32),
                   jax.ShapeDtypeStruct((depth, SSM_LANES, SSM_GROUP_CH), F32),
                   jax.ShapeDtypeStruct((depth, SSM_LANES, SSM_GROUP_CH), F32)],
        compiler_params=_cparams(("parallel",)),
        name="ssm_prep",
    )(*rows, *cols, *bs)


def _ssm_scan_kernel(seg_len, chunks_per_seg, has_h0, *refs):
    if has_h0:
        u_ref, h0_ref, wb_ref, wc_ref, pw_ref, d_ref, y_ref, h_ref, buf, carry_scr = refs
    else:
        u_ref, wb_ref, wc_ref, pw_ref, d_ref, y_ref, h_ref, buf, carry_scr = refs
    i = pl.program_id(0)
    rows = u_ref.shape[0]
    n_tiles = rows // SUBLANE
    width = 2 * SSM_LANES
    tile_shape = (SUBLANE, SSM_LANES)

    @pl.when(i == 0)
    def _():
        carry_scr[...] = jnp.zeros_like(carry_scr)

    def cmul_add(base_re, base_im, a_re, a_im, v_re, v_im):
        return (base_re + a_re * v_re - a_im * v_im, base_im + a_re * v_im + a_im * v_re)

    def pw(r0, r1):
        return pw_ref[r0:r1, 0:SSM_LANES], pw_ref[r0:r1, SSM_LANES:width]

    u = u_ref[...]
    x = _dot(u.astype(BF16), wb_ref[...])
    chain = seg_len > SUBLANE
    if has_h0:
        h0 = h0_ref[...]
        x_re, x_im = cmul_add(x[:, :SSM_LANES], x[:, SSM_LANES:], *pw(0, 1),
                              h0[:, :SSM_LANES], h0[:, SSM_LANES:])
        buf[:, 0:SSM_LANES] = x_re
        buf[:, SSM_LANES:width] = x_im
        carry0 = (jnp.zeros(tile_shape, F32), jnp.zeros(tile_shape, F32))
    else:
        buf[...] = x
        first = (i % chunks_per_seg) == 0
        carry0 = (jnp.broadcast_to(jnp.where(first, 0.0, carry_scr[:, 0:SSM_LANES]), tile_shape),
                  jnp.broadcast_to(jnp.where(first, 0.0, carry_scr[:, SSM_LANES:width]), tile_shape))

    span = min(seg_len, SUBLANE)
    row_in_seg = lax.broadcasted_iota(jnp.int32, tile_shape, 0) & (span - 1)

    def tile_step(r, carry):
        base = pl.multiple_of(r * SUBLANE, SUBLANE)
        t_re = buf[pl.ds(base, SUBLANE), 0:SSM_LANES]
        t_im = buf[pl.ds(base, SUBLANE), SSM_LANES:width]
        d = 1
        while d < span:
            keep = row_in_seg >= d
            sh_re = jnp.where(keep, pltpu.roll(t_re, d, 0), 0.0)
            sh_im = jnp.where(keep, pltpu.roll(t_im, d, 0), 0.0)
            t_re, t_im = cmul_add(t_re, t_im, *pw(d - 1, d), sh_re, sh_im)
            d *= 2
        if chain:
            t_re, t_im = cmul_add(t_re, t_im, *pw(0, SUBLANE), *carry)
            carry = (jnp.broadcast_to(t_re[SUBLANE - 1:SUBLANE, :], tile_shape),
                     jnp.broadcast_to(t_im[SUBLANE - 1:SUBLANE, :], tile_shape))
        buf[pl.ds(base, SUBLANE), 0:SSM_LANES] = t_re
        buf[pl.ds(base, SUBLANE), SSM_LANES:width] = t_im
        return carry

    c_re, c_im = lax.fori_loop(0, n_tiles, tile_step, carry0, unroll=2)

    y_ref[...] = _dot(buf[...].astype(BF16), wc_ref[...]) + d_ref[...] * u
    if has_h0:
        h_ref[...] = buf[...]
    else:
        carry_scr[:, 0:SSM_LANES] = c_re[0:1, :]
        carry_scr[:, SSM_LANES:width] = c_im[0:1, :]
        h_ref[:, 0:SSM_LANES] = c_re[0:1, :]
        h_ref[:, SSM_LANES:width] = c_im[0:1, :]


def _ssm_scan(u, h0_rows, wb, wc, pw, d_row, layer, seg_len):
    n = u.shape[0]
    rows = min(SSM_CHUNK, n)
    n_chunks = n // rows
    has_h0 = h0_rows is not None
    chunks_per_seg = max(seg_len // rows, 1)
    width = 2 * SSM_LANES
    n_pw = pw.shape[1]
    in_specs = [pl.BlockSpec((rows, SSM_WIDTH), lambda i: (i, 0))]
    args = [u]
    if has_h0:
        in_specs.append(pl.BlockSpec((rows, width), lambda i: (i, 0)))
        args.append(h0_rows)
        h_spec = pl.BlockSpec((rows, width), lambda i: (i, 0))
        h_shape = jax.ShapeDtypeStruct((n, width), F32)
    else:
        h_spec = pl.BlockSpec((None, 1, width), lambda i: (i, 0, 0))
        h_shape = jax.ShapeDtypeStruct((n_chunks, 1, width), F32)
    in_specs += [
        pl.BlockSpec((None, SSM_WIDTH, width), lambda i: (layer, 0, 0)),
        pl.BlockSpec((None, width, SSM_WIDTH), lambda i: (layer, 0, 0)),
        pl.BlockSpec((None, n_pw, width), lambda i: (layer, 0, 0)),
        pl.BlockSpec((None, 1, SSM_WIDTH), lambda i: (layer, 0, 0)),
    ]
    args += [wb, wc, pw, d_row]
    return pl.pallas_call(
        functools.partial(_ssm_scan_kernel, seg_len, chunks_per_seg, has_h0),
        grid=(n_chunks,),
        in_specs=in_specs,
        out_specs=[pl.BlockSpec((rows, SSM_WIDTH), lambda i: (i, 0)), h_spec],
        out_shape=[jax.ShapeDtypeStruct((n, SSM_WIDTH), F32), h_shape],
        scratch_shapes=[pltpu.VMEM((rows, width), F32), pltpu.VMEM((1, width), F32)],
        compiler_params=_cparams(("arbitrary",)),
        name="ssm_scan",
    )(*args)


def _mixer_out_kernel(x_ref, g_ref, olat_ref, osb_ref, y_ref, wuv_ref, wglu_ref, bglu_ref,
                      nm_ref, ns_ref, nz_ref, wout_ref, lng_ref, lnb_ref, o_ref):
    olat = olat_ref[...].astype(BF16)
    o_mla = jnp.concatenate([_dot(olat[:, 0:256], wuv_ref[0]), _dot(olat[:, 256:512], wuv_ref[1])], axis=1)
    yg = jax.nn.gelu(y_ref[...], approximate=True)
    o_ssm = yg * jax.nn.sigmoid(_dot(yg.astype(BF16), wglu_ref[...]) + bglu_ref[...])
    merged = jnp.concatenate([_rms(o_mla, nm_ref[...]), _rms(osb_ref[...], ns_ref[...]),
                              _rms(o_ssm, nz_ref[...])], axis=1).astype(BF16)
    m = _dot(merged, wout_ref[...])
    y = DEEPNORM_ALPHA * x_ref[...] + (1.0 + g_ref[...]) * m
    o_ref[...] = _layer_norm(y, lng_ref[...], lnb_ref[...])


def _mixer_out(x, g, olat, osb, y, wuv, wglu, bglu, nm, ns, nz, wout, ln_g3, ln_b3, layer, tokens_per_batch):
    n = x.shape[0]
    tm = min(512, tokens_per_batch)
    tpb = max(tokens_per_batch // tm, 1)
    ln_idx = layer * 3 + 1

    def wspec(shape):
        nd = len(shape)
        return pl.BlockSpec((None,) + shape, lambda i: (layer,) + (0,) * nd)

    return pl.pallas_call(
        _mixer_out_kernel,
        grid=(n // tm,),
        in_specs=[
            pl.BlockSpec((tm, D_MODEL), lambda i: (i, 0)),
            _mod_spec(g, tm, tpb),
            pl.BlockSpec((tm, 512), lambda i: (i, 0)),
            pl.BlockSpec((tm, SB_WIDTH), lambda i: (i, 0)),
            pl.BlockSpec((tm, SSM_WIDTH), lambda i: (i, 0)),
            wspec((2, 256, 256)), wspec((SSM_WIDTH, SSM_WIDTH)), wspec((1, SSM_WIDTH)),
            wspec((1, 512)), wspec((1, SB_WIDTH)), wspec((1, SSM_WIDTH)),
            wspec((D_MODEL, D_MODEL)),
            pl.BlockSpec((None, 1, D_MODEL), lambda i: (ln_idx, 0, 0)),
            pl.BlockSpec((None, 1, D_MODEL), lambda i: (ln_idx, 0, 0)),
        ],
        out_specs=pl.BlockSpec((tm, D_MODEL), lambda i: (i, 0)),
        out_shape=jax.ShapeDtypeStruct((n, D_MODEL), F32),
        compiler_params=_cparams(("parallel",)),
        name="mixer_out",
    )(x, g, olat, osb, y, wuv, wglu, bglu, nm, ns, nz, wout, ln_g3, ln_b3)


def _mla_decode_kernel(n_pages, n_pg, layer, pt_ref, q_ref, knew_ref, ckv_hbm, krt_hbm, o_ref,
                       ckv_buf, kr_buf, sem):
    b = pl.program_id(0)
    n_chunks = n_pages // n_pg
    rows = q_ref.shape[0]

    def page_copies(bb, c, j, slot):
        pid = pt_ref[bb * n_pages + c * n_pg + j]
        return (pltpu.make_async_copy(ckv_hbm.at[pid, layer], ckv_buf.at[slot, j], sem.at[0, slot]),
                pltpu.make_async_copy(krt_hbm.at[pid, layer], kr_buf.at[slot, j], sem.at[1, slot]))

    def start_chunk(bb, c, slot):
        def body(j2, carry):
            for parity in range(2):
                for cp in page_copies(bb, c, 2 * j2 + parity, slot):
                    cp.start(priority=parity)
            return carry
        lax.fori_loop(0, n_pg // 2, body, 0)

    def wait_chunk(bb, c, slot):
        def body(j, carry):
            for cp in page_copies(bb, c, j, slot):
                cp.wait()
            return carry
        lax.fori_loop(0, n_pg, body, 0)

    @pl.when(b == 0)
    def _():
        start_chunk(0, 0, 0)

    q = q_ref[...]
    qf = q.astype(F32)
    knew = knew_ref[...].astype(F32)
    tok = lax.shift_right_logical(lax.broadcasted_iota(jnp.int32, (rows, 1), 0), 2)
    s_new = [jnp.where(tok >= s, jnp.sum(qf * knew[s:s + 1, :], axis=1, keepdims=True) * MLA_SCALE, NEG_BIG)
             for s in range(knew.shape[0])]
    m = functools.reduce(jnp.maximum, s_new)
    l = jnp.zeros((rows, 1), F32)
    acc = jnp.zeros((rows, KV_LORA_RANK), F32)
    for s in range(knew.shape[0]):
        pexp = jnp.exp(s_new[s] - m)
        l = l + pexp
        acc = acc + pexp.astype(BF16).astype(F32) * knew[s:s + 1, 0:KV_LORA_RANK]

    q_lat = q[:, 0:KV_LORA_RANK]
    q_rope = q[:, KV_LORA_RANK:KV_LORA_RANK + MLA_ROPE_DIM]
    for c in range(n_chunks):
        slot = c % 2
        wait_chunk(b, c, slot)
        if c + 1 < n_chunks:
            start_chunk(b, c + 1, 1 - slot)
        else:
            @pl.when(b + 1 < pl.num_programs(0))
            def _():
                start_chunk(b + 1, 0, 0)

        ckv = [ckv_buf[slot, j].astype(BF16) for j in range(n_pg)]
        s = jnp.concatenate(
            [_dot_nt(q_lat, ckv[j]) + _dot(q_rope, kr_buf[slot, j].astype(BF16)) for j in range(n_pg)],
            axis=1) * MLA_SCALE
        m_new = jnp.maximum(m, jnp.max(s, axis=1, keepdims=True))
        alpha = jnp.exp(m - m_new)
        pexp = jnp.exp(s - m_new)
        l = alpha * l + jnp.sum(pexp, axis=1, keepdims=True)
        pb = pexp.astype(BF16)
        pv = _dot(pb[:, 0:PAGE_SIZE], ckv[0])
        for j in range(1, n_pg):
            pv = pv + _dot(pb[:, j * PAGE_SIZE:(j + 1) * PAGE_SIZE], ckv[j])
        acc = alpha * acc + pv
        m = m_new
    o_ref[...] = acc / l


def _mla_decode(page_table_flat, q16, knew, cache_ckv, cache_krope_t, layer, n_pages):
    nb, rows, _ = q16.shape
    n_pg = min(DEC_PAGES, n_pages // 2)
    assert n_pages % (2 * n_pg) == 0 and n_pg % 2 == 0
    grid_spec = pltpu.PrefetchScalarGridSpec(
        num_scalar_prefetch=1,
        grid=(nb,),
        in_specs=[pl.BlockSpec((None, rows, 256), lambda b, pt: (b, 0, 0)),
                  pl.BlockSpec((None, knew.shape[1], 256), lambda b, pt: (b, 0, 0)),
                  pl.BlockSpec(memory_space=pl.ANY),
                  pl.BlockSpec(memory_space=pl.ANY)],
        out_specs=pl.BlockSpec((None, rows, KV_LORA_RANK), lambda b, pt: (b, 0, 0)),
        scratch_shapes=[pltpu.VMEM((2, n_pg, PAGE_SIZE, KV_LORA_RANK), F32),
                        pltpu.VMEM((2, n_pg, MLA_ROPE_DIM, PAGE_SIZE), F32),
                        pltpu.SemaphoreType.DMA((2, 2))],
    )
    return pl.pallas_call(
        functools.partial(_mla_decode_kernel, n_pages, n_pg, layer),
        grid_spec=grid_spec,
        out_shape=jax.ShapeDtypeStruct((nb, rows, KV_LORA_RANK), F32),
        compiler_params=_cparams(("arbitrary",)),
        name="mla_decode",
    )(page_table_flat, q16, knew, cache_ckv, cache_krope_t)


SB_DEC_ROWS = SB_HEADS * SUBLANE
SB_DEC_SLOTS = 4
SB_DEC_AHEAD = SB_DEC_SLOTS - 1


def _sb_decode_kernel(n_pages, layer, pt_ref, q_ref, knew_ref, vnew_ref, k_hbm, v_hbm, o_ref,
                      kbuf, vbuf, sem, c_scr, acc_scr):
    b = pl.program_id(0)

    def copies(p, slot, bb=b):
        pid = pt_ref[bb * n_pages + p]
        return (pltpu.make_async_copy(k_hbm.at[pid, layer], kbuf.at[slot], sem.at[0, slot]),
                pltpu.make_async_copy(v_hbm.at[pid, layer], vbuf.at[slot], sem.at[1, slot]))

    def start(p, slot, bb=b):
        for cp in copies(p, slot, bb):
            cp.start()

    def wait(p, slot):
        for cp in copies(p, slot):
            cp.wait()

    def slot_of(p):
        return (n_pages - 1 - p) & (SB_DEC_SLOTS - 1)

    def prime(bb):
        for d in range(SB_DEC_AHEAD):
            start(n_pages - 1 - d, d, bb)

    @pl.when(b == 0)
    def _():
        prime(b)

    q = q_ref[...]
    qf = q.astype(F32)
    tok = lax.broadcasted_iota(jnp.int32, (SB_DEC_ROWS, 1), 0) & (SUBLANE - 1)
    n_new = knew_ref.shape[0]
    knew = knew_ref[...].astype(F32)
    vnew = vnew_ref[...].astype(F32)
    carry = jnp.where(tok < n_new, 0.0, NEG_BIG)
    acc = jnp.zeros((SB_DEC_ROWS, SB_WIDTH), F32)
    for s in range(n_new - 1, -1, -1):
        z = jnp.sum(qf * knew[s:s + 1, :], axis=1, keepdims=True)
        log_beta, lk = _sb_logs(z)
        allowed = tok > s
        w = jnp.where(allowed, jnp.exp(log_beta + carry), 0.0)
        carry = carry + jnp.where(allowed, lk, 0.0)
        acc = acc + w.astype(BF16).astype(F32) * vnew[s:s + 1, :]
    c_scr[...] = carry
    acc_scr[...] = acc
    umat = _suffix_ones(PAGE_SIZE)

    def alive():
        return (jnp.max(c_scr[...]) >= SB_DEAD_LOG).astype(jnp.int32)

    def cond(state):
        p, live = state
        return jnp.logical_and(p >= 0, live > 0)

    def body(state):
        p, _ = state
        slot = slot_of(p)
        wait(p, slot)

        @pl.when(p >= SB_DEC_AHEAD)
        def _():
            start(p - SB_DEC_AHEAD, slot_of(p - SB_DEC_AHEAD))

        kpage_t = kbuf[slot].astype(BF16)
        vpage_t = vbuf[slot].astype(BF16)
        z = _dot(q, kpage_t)
        w, c_new = _sb_block(z, c_scr[...], umat, None)
        c_scr[...] = c_new
        acc_scr[...] += _dot_nt(w.astype(BF16), vpage_t)
        return p - 1, alive()

    p_end, _ = lax.while_loop(cond, body, (n_pages - 1, alive()))

    for d in range(SB_DEC_AHEAD):
        @pl.when(p_end - d >= 0)
        def _(d=d):
            wait(p_end - d, slot_of(p_end - d))

    @pl.when(b + 1 < pl.num_programs(0))
    def _():
        prime(b + 1)

    lane_head = _lane_head((SUBLANE, SB_WIDTH))
    out = jnp.zeros((SUBLANE, SB_WIDTH), F32)
    for hd in range(SB_HEADS):
        out = out + jnp.where(lane_head == hd, acc_scr[hd * SUBLANE:(hd + 1) * SUBLANE, :], 0.0)
    o_ref[...] = out


def _sb_decode(page_table_flat, q32, knew, vnew, cache_k, cache_v, layer, n_pages):
    nb = q32.shape[0]
    n_new = knew.shape[1]
    grid_spec = pltpu.PrefetchScalarGridSpec(
        num_scalar_prefetch=1,
        grid=(nb,),
        in_specs=[pl.BlockSpec((None, SB_DEC_ROWS, SB_WIDTH), lambda b, pt: (b, 0, 0)),
                  pl.BlockSpec((None, n_new, SB_WIDTH), lambda b, pt: (b, 0, 0)),
                  pl.BlockSpec((None, n_new, SB_WIDTH), lambda b, pt: (b, 0, 0)),
                  pl.BlockSpec(memory_space=pl.ANY),
                  pl.BlockSpec(memory_space=pl.ANY)],
        out_specs=pl.BlockSpec((None, SUBLANE, SB_WIDTH), lambda b, pt: (b, 0, 0)),
        scratch_shapes=[pltpu.VMEM((SB_DEC_SLOTS, SB_WIDTH, PAGE_SIZE), F32),
                        pltpu.VMEM((SB_DEC_SLOTS, SB_WIDTH, PAGE_SIZE), F32),
                        pltpu.SemaphoreType.DMA((2, SB_DEC_SLOTS)),
                        pltpu.VMEM((SB_DEC_ROWS, 1), F32), pltpu.VMEM((SB_DEC_ROWS, SB_WIDTH), F32)],
    )
    return pl.pallas_call(
        functools.partial(_sb_decode_kernel, n_pages, layer),
        grid_spec=grid_spec,
        out_shape=jax.ShapeDtypeStruct((nb, SUBLANE, SB_WIDTH), F32),
        compiler_params=_cparams(("arbitrary",)),
        name="sb_decode",
    )(page_table_flat, q32, knew, vnew, cache_k, cache_v)


def _swap_halves(w):
    half = w.shape[-1] // 2
    return jnp.concatenate([w[..., half:], w[..., :half]], axis=-1)


def _block_diag_pairs(w):
    depth, c, heads, d = w.shape
    wt = jnp.transpose(w, (0, 2, 3, 1)).reshape(depth, heads // 2, 2, d, c)
    eye = jnp.eye(2, dtype=w.dtype)
    return jnp.einsum('lpadc,ab->lpadbc', wt, eye).reshape(depth, heads // 2, 2 * d, 2 * c)


def _rope_table(pos):
    half = MLA_ROPE_DIM // 2
    freqs = ROPE_BASE ** (-jnp.arange(half, dtype=F32) / half)
    ang = pos.astype(F32)[:, None] * freqs[None, :]
    cos, sin = jnp.cos(ang), jnp.sin(ang)
    return jnp.concatenate([cos, cos, -sin, sin], axis=-1)


def kernel(x_prompt, x_sample, c_prompt, c_sample, page_table, cache_ckv, cache_krope, cache_sb_k, cache_sb_v, state_ssm_re, state_ssm_im, ada_w, ada_b, ffn1_w_gate, ffn1_w_up, ffn1_w_down, ffn2_w_gate, ffn2_w_up, ffn2_w_down, ln_g, ln_b, w_in, mla_q_norm, mla_kv_norm, mla_w_uq, mla_w_uk, mla_w_uv, ssm_lam_re, ssm_lam_im, ssm_log_dt, ssm_b_re, ssm_b_im, ssm_c_re, ssm_c_im, ssm_d, ssm_w_glu, ssm_b_glu, out_norm_mla, out_norm_sb, out_norm_ssm, w_out):
    batch, seq, _ = x_prompt.shape
    dec_batch, dec_seq, _ = x_sample.shape
    n_pages = page_table.shape[1]
    past_len = n_pages * PAGE_SIZE
    n_p, n_s = batch * seq, dec_batch * dec_seq
    depth = ada_w.shape[0]
    assert seq % SSM_CHUNK == 0 and seq % MLA_TQ == 0 and seq % MLA_TK == 0 and seq % SB_TQ == 0
    assert n_pages % 2 == 0 and n_pages >= SB_DEC_AHEAD and dec_seq < SUBLANE and (dec_seq & (dec_seq - 1)) == 0
    assert n_s <= SSM_CHUNK and n_s % SUBLANE == 0

    bf = lambda a: a.astype(BF16)
    ffn_w = [(bf(ffn1_w_gate), bf(ffn1_w_up), bf(ffn1_w_down)), (bf(ffn2_w_gate), bf(ffn2_w_up), bf(ffn2_w_down))]
    kr_cols = w_in[:, :, 384:448]
    win_ext = bf(jnp.concatenate([w_in[:, :, 0:448], _swap_halves(kr_cols), w_in[:, :, 448:]], axis=-1))
    nope = mla_w_uq[..., :MLA_NOPE_DIM].reshape(depth, Q_LORA_RANK, MLA_HEADS * MLA_NOPE_DIM)
    rope_w = mla_w_uq[..., MLA_NOPE_DIM:]
    rope_pairs = jnp.concatenate([rope_w, _swap_halves(rope_w)], axis=-1).reshape(depth, Q_LORA_RANK, MLA_HEADS * 128)
    wq = bf(jnp.concatenate([nope, rope_pairs], axis=-1))
    wuk_bd = bf(_block_diag_pairs(mla_w_uk))
    wuv_bd = bf(_block_diag_pairs(jnp.transpose(mla_w_uv, (0, 3, 2, 1))))
    wglu = bf(ssm_w_glu)
    wout = bf(w_out)
    row = lambda a: a.reshape(depth, 1, a.shape[-1])
    qn, kvn = row(mla_q_norm), row(mla_kv_norm)
    nm, ns, nz, bglu = row(out_norm_mla), row(out_norm_sb), row(out_norm_ssm), row(ssm_b_glu)
    d_row = ssm_d.reshape(depth, 1, SSM_WIDTH)
    ln_g3 = ln_g.reshape(depth * 3, 1, D_MODEL)
    ln_b3 = ln_b.reshape(depth * 3, 1, D_MODEL)

    pw, bb_re, bb_im = _ssm_prep(ssm_lam_re, ssm_lam_im, ssm_log_dt, ssm_b_re, ssm_b_im, SUBLANE)
    eye_g = jnp.eye(SSM_GROUPS, dtype=F32)

    def b_map(bb):
        bb = bb.reshape(depth, SSM_GROUPS, SSM_STATE, SSM_GROUP_CH)
        return jnp.einsum('lgph,gk->lghkp', bb, eye_g).reshape(depth, SSM_WIDTH, SSM_LANES)

    def c_map(cc):
        return jnp.einsum('lghp,gk->lgpkh', cc, eye_g).reshape(depth, SSM_LANES, SSM_WIDTH)

    wb = bf(jnp.concatenate([b_map(bb_re), b_map(bb_im)], axis=-1))
    wc = bf(jnp.concatenate([c_map(ssm_c_re), -c_map(ssm_c_im)], axis=1))

    mod = _ada(jnp.concatenate([c_prompt, c_sample], axis=0), ada_w, ada_b)
    tab_p = _rope_table(jnp.arange(seq))
    tab_s = jnp.tile(_rope_table(past_len + jnp.arange(dec_seq)), (dec_batch, 1))
    pt_flat = page_table.reshape(-1).astype(jnp.int32)
    cache_krope_t = jnp.swapaxes(cache_krope, 2, 3)
    sb_page_t = lambda c: jnp.transpose(c, (0, 1, 3, 4, 2)).reshape(c.shape[:2] + (SB_WIDTH, PAGE_SIZE))
    cache_k4, cache_v4 = sb_page_t(cache_sb_k), sb_page_t(cache_sb_v)
    head_mask = (jnp.arange(SB_WIDTH)[None, :] // SB_HEAD_DIM == jnp.arange(SB_HEADS)[:, None])

    xp = x_prompt.reshape(n_p, D_MODEL)
    xs = x_sample.reshape(n_s, D_MODEL)
    st_p, st_s = [], []
    for l in range(depth):
        mp = mod[l, :batch].reshape(batch, N_MOD, 1, D_MODEL)
        ms = jnp.repeat(mod[l, batch:], dec_seq, axis=0).reshape(n_s, N_MOD, D_MODEL)
        mod_p = [mp[:, k] for k in range(N_MOD)]
        mod_s = [ms[:, k] for k in range(N_MOD)]

        xp = _ffn(xp, mod_p[0], mod_p[1], mod_p[2], *ffn_w[0], ln_g3, ln_b3, l, 0, seq)
        q, kpad, ckv, kr, sq, sk, sv, skb, svb, u = _mixer_in(
            xp, mod_p[3], mod_p[4], tab_p, seq, win_ext, qn, kvn, wq, wuk_bd, l, seq)
        olat = _mla_prompt(q, kpad, batch, seq)
        osb = _sb_prompt(sq, skb, svb, batch, seq)
        y, h_last = _ssm_scan(u, None, wb, wc, pw, d_row, l, seq)
        xp = _mixer_out(xp, mod_p[5], olat, osb, y, wuv_bd, wglu, bglu, nm, ns, nz, wout, ln_g3, ln_b3, l, seq)
        xp = _ffn(xp, mod_p[6], mod_p[7], mod_p[8], *ffn_w[1], ln_g3, ln_b3, l, 2, seq)
        cps = seq // SSM_CHUNK
        h_fin = h_last.reshape(batch, cps, 2 * SSM_LANES)[:, cps - 1]
        st_p.append((ckv.reshape(batch, seq, KV_LORA_RANK), kr.reshape(batch, seq, MLA_ROPE_DIM),
                     sk.reshape(batch, seq, SB_HEADS, SB_HEAD_DIM), sv.reshape(batch, seq, SB_HEADS, SB_HEAD_DIM),
                     h_fin[:, :SSM_LANES].reshape(batch, SSM_GROUPS, SSM_STATE),
                     h_fin[:, SSM_LANES:].reshape(batch, SSM_GROUPS, SSM_STATE)))

        xs = _ffn(xs, mod_s[0], mod_s[1], mod_s[2], *ffn_w[0], ln_g3, ln_b3, l, 0, n_s)
        q, kpad, ckv, kr, sq, sk, sv, skb, svb, u = _mixer_in(
            xs, mod_s[3], mod_s[4], tab_s, n_s, win_ext, qn, kvn, wq, wuk_bd, l, n_s)
        q16 = q.reshape(dec_batch, dec_seq * MLA_HEADS, 256)
        olat = _mla_decode(pt_flat, q16, kpad.reshape(dec_batch, dec_seq, 256), cache_ckv, cache_krope_t, l, n_pages)
        olat = olat.reshape(n_s, MLA_HEADS * KV_LORA_RANK)
        sq3 = sq.reshape(dec_batch, 1, dec_seq, SB_WIDTH)
        q32 = jnp.where(head_mask[None, :, None, :], sq3, jnp.zeros_like(sq3))
        q32 = jnp.pad(q32, ((0, 0), (0, 0), (0, SUBLANE - dec_seq), (0, 0))).reshape(dec_batch, SB_DEC_ROWS, SB_WIDTH)
        osb = _sb_decode(pt_flat, q32, skb.reshape(dec_batch, dec_seq, SB_WIDTH),
                         svb.reshape(dec_batch, dec_seq, SB_WIDTH), cache_k4, cache_v4, l, n_pages)
        osb = osb[:, :dec_seq].reshape(n_s, SB_WIDTH)
        h0 = jnp.concatenate([state_ssm_re[:, l].reshape(dec_batch, 1, SSM_LANES),
                              state_ssm_im[:, l].reshape(dec_batch, 1, SSM_LANES)], axis=-1)
        h0_rows = jnp.pad(h0, ((0, 0), (0, dec_seq - 1), (0, 0))).reshape(n_s, 2 * SSM_LANES)
        y, h_all = _ssm_scan(u, h0_rows, wb, wc, pw, d_row, l, dec_seq)
        xs = _mixer_out(xs, mod_s[5], olat, osb, y, wuv_bd, wglu, bglu, nm, ns, nz, wout, ln_g3, ln_b3, l, n_s)
        xs = _ffn(xs, mod_s[6], mod_s[7], mod_s[8], *ffn_w[1], ln_g3, ln_b3, l, 2, n_s)
        h_fin = h_all.reshape(dec_batch, dec_seq, 2 * SSM_LANES)[:, dec_seq - 1]
        st_s.append((ckv.reshape(dec_batch, dec_seq, KV_LORA_RANK), kr.reshape(dec_batch, dec_seq, MLA_ROPE_DIM),
                     sk.reshape(dec_batch, dec_seq, SB_HEADS, SB_HEAD_DIM),
                     sv.reshape(dec_batch, dec_seq, SB_HEADS, SB_HEAD_DIM),
                     h_fin[:, :SSM_LANES].reshape(dec_batch, SSM_GROUPS, SSM_STATE),
                     h_fin[:, SSM_LANES:].reshape(dec_batch, SSM_GROUPS, SSM_STATE)))

    stack = lambda sts, i: jnp.stack([s[i] for s in sts], axis=1)
    return (xp.reshape(batch, seq, D_MODEL), xs.reshape(dec_batch, dec_seq, D_MODEL),
            stack(st_p, 0), stack(st_p, 1), stack(st_p, 2), stack(st_p, 3), stack(st_p, 4), stack(st_p, 5),
            stack(st_s, 0), stack(st_s, 1), stack(st_s, 2), stack(st_s, 3), stack(st_s, 4), stack(st_s, 5))
```
